```python
import math
import jax
import jax.numpy as jnp
from jax import lax
import numpy as np

D_MODEL = 2048
BATCH = 4
SEQ = 8192
DEPTH = 4
DEC_BATCH = 16
DEC_SEQ = 32
PAST_LEN = 1024

CHUNK = 64
HEAD_DIM = 128
N_MIXERS = 3
N_A = (DEPTH + N_MIXERS - 1) // N_MIXERS
N_B = (DEPTH + N_MIXERS - 2) // N_MIXERS
N_C = DEPTH // N_MIXERS
A_HEADS = D_MODEL // HEAD_DIM
A_KV_HEADS = 4
A_GROUP = A_HEADS // A_KV_HEADS
WINDOW = 128
WINDOW_CHUNKS = WINDOW // CHUNK
BAND = (WINDOW_CHUNKS + 1) * CHUNK
A_CACHE = min(WINDOW, PAST_LEN)
B_HEADS = D_MODEL // (2 * HEAD_DIM)
B_VDIM = 2 * HEAD_DIM
C_HEADS = D_MODEL // HEAD_DIM
N_BUCKETS = 32
MAX_DISTANCE = 128
BIAS_HEADS = A_HEADS
N_GROUPS = 4
EXPERTS_PER_GROUP = 4
N_EXPERTS = N_GROUPS * EXPERTS_PER_GROUP
TOP_K = 2
D_EXPERT = D_MODEL // 4
Q_BLOCK = 128
EPS = 1e-6

kernel_name = 'hybrid_streaming_encoder_step'


def rms_norm(x, gain):
    xf = x.astype(jnp.float32)
    y = xf * lax.rsqrt(jnp.mean(xf * xf, axis=-1, keepdims=True) + EPS)
    return (y * gain.astype(jnp.float32)).astype(x.dtype)


def t5_bucket(rel):
    half = N_BUCKETS // 2
    max_exact = half // 2
    n = jnp.abs(rel)
    nf = jnp.maximum(n, 1).astype(jnp.float32)
    large = max_exact + (jnp.log(nf / max_exact) / math.log(MAX_DISTANCE / max_exact)
                         * (half - max_exact)).astype(jnp.int32)
    large = jnp.minimum(large, half - 1)
    return jnp.where(rel > 0, half, 0) + jnp.where(n < max_exact, n, large)


def rel_bias(table, q_pos, k_pos):
    b = t5_bucket(k_pos[None, :] - q_pos[:, None])
    return jnp.moveaxis(table[b].astype(jnp.float32), -1, 0)


def sink_softmax(s, sink):
    sk = sink.astype(jnp.float32)[:, :, None, None]
    m = jnp.maximum(jnp.max(s, axis=-1, keepdims=True), sk)
    e = jnp.exp(s - m)
    return e / (jnp.sum(e, axis=-1, keepdims=True) + jnp.exp(sk - m))


def proj_a(h, w_in, gq, gk):
    bsz, slen, _ = h.shape
    q, k, v = jnp.split(h @ w_in, [A_HEADS * HEAD_DIM, (A_HEADS + A_KV_HEADS) * HEAD_DIM], axis=-1)
    q = rms_norm(q.reshape(bsz, slen, A_KV_HEADS, A_GROUP, HEAD_DIM), gq)
    k = rms_norm(k.reshape(bsz, slen, A_KV_HEADS, HEAD_DIM), gk)
    v = v.reshape(bsz, slen, A_KV_HEADS, HEAD_DIM)
    return q, k, v


def mixer_a_prompt(h, w_in, gq, gk, sinks, w_out, table):
    bsz, slen, _ = h.shape
    n = slen // CHUNK
    q, k, v = proj_a(h, w_in, gq, gk)
    pad = ((0, 0), (WINDOW_CHUNKS * CHUNK, 0), (0, 0), (0, 0))
    kp = jnp.pad(k, pad).reshape(bsz, n + WINDOW_CHUNKS, CHUNK, A_KV_HEADS, HEAD_DIM)
    vp = jnp.pad(v, pad).reshape(bsz, n + WINDOW_CHUNKS, CHUNK, A_KV_HEADS, HEAD_DIM)
    kb = jnp.concatenate([kp[:, j:j + n] for j in range(WINDOW_CHUNKS + 1)], axis=2)
    vb = jnp.concatenate([vp[:, j:j + n] for j in range(WINDOW_CHUNKS + 1)], axis=2)
    qb = q.reshape(bsz, n, CHUNK, A_KV_HEADS, A_GROUP, HEAD_DIM)
    s = jnp.einsum('bnqhgd,bnshd->bnhgqs', qb, kb).astype(jnp.float32) * HEAD_DIM ** -0.5
    k_loc = jnp.arange(BAND)
    q_loc = WINDOW_CHUNKS * CHUNK + jnp.arange(CHUNK)
    bias = rel_bias(table, q_loc, k_loc).reshape(A_KV_HEADS, A_GROUP, CHUNK, BAND)
    k_pos = jnp.arange(n)[:, None] * CHUNK - WINDOW_CHUNKS * CHUNK + k_loc[None, :]
    valid = (k_pos >= 0)[None, :, None, None, None, :]
    p = sink_softmax(jnp.where(valid, s + bias, -jnp.inf), sinks.reshape(A_KV_HEADS, A_GROUP))
    o = jnp.einsum('bnhgqs,bnshd->bnqhgd', p.astype(v.dtype), vb).reshape(bsz, slen, D_MODEL)
    return o @ w_out, (k[:, -A_CACHE:], v[:, -A_CACHE:])


def mixer_a_sample(h, ck, cv, w_in, gq, gk, sinks, w_out, table):
    bsz, t, _ = h.shape
    q, k, v = proj_a(h, w_in, gq, gk)
    kk = jnp.concatenate([ck, k], axis=1)
    vv = jnp.concatenate([cv, v], axis=1)
    q_pos = PAST_LEN + jnp.arange(t)
    k_pos = jnp.concatenate([PAST_LEN - A_CACHE + jnp.arange(A_CACHE), q_pos])
    qc, kc = q_pos[:, None] // CHUNK, k_pos[None, :] // CHUNK
    valid = (kc <= qc) & (kc >= qc - WINDOW_CHUNKS)
    s = jnp.einsum('bqhgd,bshd->bhgqs', q, kk).astype(jnp.float32) * HEAD_DIM ** -0.5
    bias = rel_bias(table, q_pos, k_pos).reshape(A_KV_HEADS, A_GROUP, t, A_CACHE + t)
    p = sink_softmax(jnp.where(valid, s + bias, -jnp.inf), sinks.reshape(A_KV_HEADS, A_GROUP))
    o = jnp.einsum('bhgqs,bshd->bqhgd', p.astype(vv.dtype), vv).reshape(bsz, t, D_MODEL)
    return o @ w_out, (kk[:, -A_CACHE:], vv[:, -A_CACHE:])


def proj_b(h, w_in, gq, gk):
    bsz, slen, _ = h.shape
    qd = B_HEADS * 2 * HEAD_DIM
    q, k, v = jnp.split(h @ w_in, [qd, 2 * qd], axis=-1)
    q = rms_norm(q.reshape(bsz, slen, B_HEADS, 2, HEAD_DIM), gq)
    k = rms_norm(k.reshape(bsz, slen, B_HEADS, 2, HEAD_DIM), gk)
    v = v.reshape(bsz, slen, B_HEADS, B_VDIM)
    return q, k, v


def diff_lambda(lq1, lk1, lq2, lk2, lam_init):
    f = lambda a, b: jnp.exp(jnp.sum(a.astype(jnp.float32) * b.astype(jnp.float32)))
    return f(lq1, lk1) - f(lq2, lk2) + lam_init


def diff_attend(q, k, v, bias, valid, lam):
    s = jnp.einsum('bqhjd,bshjd->bhjqs', q, k).astype(jnp.float32) * HEAD_DIM ** -0.5 + bias
    p = jax.nn.softmax(jnp.where(valid, s, -jnp.inf), axis=-1)
    w = p[:, :, 0] - lam * p[:, :, 1]
    return jnp.einsum('bhqs,bshe->bqhe', w.astype(v.dtype), v)


def finish_b(o, g_sub, lam_init, w_out):
    bsz, slen = o.shape[:2]
    o = rms_norm(o, g_sub) * (1.0 - lam_init)
    return o.reshape(bsz, slen, D_MODEL) @ w_out


def mixer_b_prompt(h, w_in, gq, gk, lq1, lk1, lq2, lk2, g_sub, w_out, table, lam_init):
    bsz, slen, _ = h.shape
    q, k, v = proj_b(h, w_in, gq, gk)
    lam = diff_lambda(lq1, lk1, lq2, lk2, lam_init)
    nb = slen // Q_BLOCK
    qb = jnp.moveaxis(q.reshape(bsz, nb, Q_BLOCK, B_HEADS, 2, HEAD_DIM), 1, 0)
    k_pos = jnp.arange(slen)

    def block(args):
        qi, i = args
        q_pos = i * Q_BLOCK + jnp.arange(Q_BLOCK)
        valid = (k_pos[None, :] // CHUNK) <= (q_pos[:, None] // CHUNK)
        bias = rel_bias(table, q_pos, k_pos).reshape(B_HEADS, 2, Q_BLOCK, slen)
        return diff_attend(qi, k, v, bias, valid, lam)

    o = lax.map(block, (qb, jnp.arange(nb)))
    o = jnp.moveaxis(o, 0, 1).reshape(bsz, slen, B_HEADS, B_VDIM)
    return finish_b(o, g_sub, lam_init, w_out), (k, v)


def mixer_b_sample(h, ck, cv, w_in, gq, gk, lq1, lk1, lq2, lk2, g_sub, w_out, table, lam_init):
    t = h.shape[1]
    q, k, v = proj_b(h, w_in, gq, gk)
    lam = diff_lambda(lq1, lk1, lq2, lk2, lam_init)
    kk = jnp.concatenate([ck, k], axis=1)
    vv = jnp.concatenate([cv, v], axis=1)
    q_pos = PAST_LEN + jnp.arange(t)
    k_pos = jnp.arange(PAST_LEN + t)
    valid = (k_pos[None, :] // CHUNK) <= (q_pos[:, None] // CHUNK)
    bias = rel_bias(table, q_pos, k_pos).reshape(B_HEADS, 2, t, PAST_LEN + t)
    o = diff_attend(q, kk, vv, bias, valid, lam)
    return finish_b(o, g_sub, lam_init, w_out), (k, v)


def proj_c(h, w_in):
    bsz, slen, _ = h.shape
    q, k, v = jnp.split(h @ w_in, 3, axis=-1)
    shp = (bsz, slen, C_HEADS, HEAD_DIM)
    return q.reshape(shp), k.reshape(shp), v.reshape(shp)


def stick_break(q, k, v, q_pos, k_pos):
    z = jnp.einsum('bqhd,bshd->bhqs', q, k).astype(jnp.float32) * HEAD_DIM ** -0.5
    valid = k_pos[None, :] < q_pos[:, None]
    log_1m = jnp.where(valid, jax.nn.log_sigmoid(-z), 0.0)
    after = lax.cumsum(log_1m, axis=log_1m.ndim - 1, reverse=True) - log_1m
    a = jnp.where(valid, jnp.exp(jax.nn.log_sigmoid(z) + after), 0.0)
    return jnp.einsum('bhqs,bshd->bqhd', a.astype(v.dtype), v)


def mixer_c_prompt(h, w_in, w_out):
    bsz, slen, _ = h.shape
    q, k, v = proj_c(h, w_in)
    nb = slen // Q_BLOCK
    qb = jnp.moveaxis(q.reshape(bsz, nb, Q_BLOCK, C_HEADS, HEAD_DIM), 1, 0)
    k_pos = jnp.arange(slen)

    def block(args):
        qi, i = args
        return stick_break(qi, k, v, i * Q_BLOCK + jnp.arange(Q_BLOCK), k_pos)

    o = lax.map(block, (qb, jnp.arange(nb)))
    o = jnp.moveaxis(o, 0, 1).reshape(bsz, slen, D_MODEL)
    return o @ w_out, (k, v)


def mixer_c_sample(h, ck, cv, w_in, w_out):
    bsz, t, _ = h.shape
    q, k, v = proj_c(h, w_in)
    kk = jnp.concatenate([ck, k], axis=1)
    vv = jnp.concatenate([cv, v], axis=1)
    o = stick_break(q, kk, vv, PAST_LEN + jnp.arange(t), jnp.arange(PAST_LEN + t))
    return o.reshape(bsz, t, D_MODEL) @ w_out, (k, v)


def moe(h, w_group, b_group, w_router, b_router, w_gate, w_up, w_down):
    lead = h.shape[:-1]
    t = h.reshape(-1, D_MODEL)
    p_group = jax.nn.softmax((t @ w_group + b_group).astype(jnp.float32), axis=-1)
    onehot_g = jax.nn.one_hot(jnp.argmax(p_group, axis=-1), N_GROUPS, dtype=jnp.float32)
    g_gate = jnp.sum(p_group * onehot_g, axis=-1, keepdims=True)
    logits_e = (jnp.einsum('td,gde->tge', t, w_router) + b_router).astype(jnp.float32)
    le = jnp.einsum('tge,tg->te', logits_e, onehot_g)
    top_v, top_i = lax.top_k(le, TOP_K)
    top_w = jax.nn.softmax(top_v, axis=-1) * g_gate
    w_e = jnp.sum(jax.nn.one_hot(top_i, EXPERTS_PER_GROUP, dtype=jnp.float32) * top_w[..., None], axis=1)
    combine = (onehot_g[:, :, None] * w_e[:, None, :]).reshape(-1, N_EXPERTS).astype(t.dtype)
    out = jnp.zeros_like(t)
    for g in range(N_GROUPS):
        sl = slice(g * EXPERTS_PER_GROUP, (g + 1) * EXPERTS_PER_GROUP)
        hid = jax.nn.silu(jnp.einsum('td,edf->tef', t, w_gate[sl])) * jnp.einsum('td,edf->tef', t, w_up[sl])
        out = out + jnp.einsum('tef,te,efd->td', hid, combine[:, sl], w_down[sl])
    return out.reshape(*lead, D_MODEL)


def lambda_init(layer):
    return 0.8 - 0.6 * math.exp(-0.3 * layer)


def trunk(x, c, norm_mix, norm_ffn, w_ada, b_ada, moe_w, mix):
    states = []
    for l in range(DEPTH):
        mod = (jax.nn.silu(c) @ w_ada[l] + b_ada[l])[:, None, :]
        sh1, sc1, g1, sh2, sc2, g2 = jnp.split(mod, 6, axis=-1)
        out, st = mix(l, rms_norm(x, norm_mix[l]) * (1 + sc1) + sh1)
        x = x + g1 * out
        x = x + g2 * moe(rms_norm(x, norm_ffn[l]) * (1 + sc2) + sh2, *[w[l] for w in moe_w])
        states.append(st)
    return x, states


def collect(states, kind, j):
    return jnp.stack([states[l][j] for l in range(DEPTH) if l % N_MIXERS == kind])


def setup_inputs(seed: int = 0) -> dict:
    key = jax.random.key(seed)
    key_list = jax.random.split(key, 40)
    counter = [0]

    def rnd(shape, scale):
        k = key_list[counter[0]]
        counter[0] += 1
        return jax.random.normal(k, shape, jnp.float32) * scale

    def gain(shape):
        return 1.0 + rnd(shape, 0.1)

    d_in = D_MODEL ** -0.5
    a_width = (A_HEADS + 2 * A_KV_HEADS) * HEAD_DIM
    b_width = 3 * B_HEADS * 2 * HEAD_DIM
    return {
        'x_prompt': rnd((BATCH, SEQ, D_MODEL), 1.0),
        'x_sample': rnd((DEC_BATCH, DEC_SEQ, D_MODEL), 1.0),
        'c_prompt': rnd((BATCH, D_MODEL), 1.0),
        'c_sample': rnd((DEC_BATCH, D_MODEL), 1.0),
        'cache_a_k': rnd((N_A, DEC_BATCH, A_CACHE, A_KV_HEADS, HEAD_DIM), 1.0),
        'cache_a_v': rnd((N_A, DEC_BATCH, A_CACHE, A_KV_HEADS, HEAD_DIM), 1.0),
        'cache_b_k': rnd((N_B, DEC_BATCH, PAST_LEN, B_HEADS, 2, HEAD_DIM), 1.0),
        'cache_b_v': rnd((N_B, DEC_BATCH, PAST_LEN, B_HEADS, B_VDIM), 1.0),
        'cache_c_k': rnd((N_C, DEC_BATCH, PAST_LEN, C_HEADS, HEAD_DIM), 1.0),
        'cache_c_v': rnd((N_C, DEC_BATCH, PAST_LEN, C_HEADS, HEAD_DIM), 1.0),
        'rel_bias_table': rnd((N_BUCKETS, BIAS_HEADS), 0.5),
        'norm_mix': gain((DEPTH, D_MODEL)),
        'norm_ffn': gain((DEPTH, D_MODEL)),
        'w_ada': rnd((DEPTH, D_MODEL, 6 * D_MODEL), 0.5 * d_in),
        'b_ada': rnd((DEPTH, 6 * D_MODEL), 0.02),
        'w_in_a': rnd((N_A, D_MODEL, a_width), d_in),
        'q_gain_a': gain((N_A, HEAD_DIM)),
        'k_gain_a': gain((N_A, HEAD_DIM)),
        'sinks_a': rnd((N_A, A_HEADS), 1.0),
        'w_out_a': rnd((N_A, D_MODEL, D_MODEL), d_in),
        'w_in_b': rnd((N_B, D_MODEL, b_width), d_in),
        'q_gain_b': gain((N_B, HEAD_DIM)),
        'k_gain_b': gain((N_B, HEAD_DIM)),
        'lam_q1': rnd((N_B, HEAD_DIM), 0.1),
        'lam_k1': rnd((N_B, HEAD_DIM), 0.1),
        'lam_q2': rnd((N_B, HEAD_DIM), 0.1),
        'lam_k2': rnd((N_B, HEAD_DIM), 0.1),
        'sub_gain_b': gain((N_B, B_VDIM)),
        'w_out_b': rnd((N_B, D_MODEL, D_MODEL), d_in),
        'w_in_c': rnd((N_C, D_MODEL, 3 * D_MODEL), d_in),
        'w_out_c': rnd((N_C, D_MODEL, D_MODEL), d_in),
        'w_group': rnd((DEPTH, D_MODEL, N_GROUPS), d_in),
        'b_group': rnd((DEPTH, N_GROUPS), 0.01),
        'w_router': rnd((DEPTH, N_GROUPS, D_MODEL, EXPERTS_PER_GROUP), d_in),
        'b_router': rnd((DEPTH, N_GROUPS, EXPERTS_PER_GROUP), 0.01),
        'w_gate': rnd((DEPTH, N_EXPERTS, D_MODEL, D_EXPERT), d_in),
        'w_up': rnd((DEPTH, N_EXPERTS, D_MODEL, D_EXPERT), d_in),
        'w_down': rnd((DEPTH, N_EXPERTS, D_EXPERT, D_MODEL), D_EXPERT ** -0.5),
    }


def reference(x_prompt, x_sample, c_prompt, c_sample,
              cache_a_k, cache_a_v, cache_b_k, cache_b_v, cache_c_k, cache_c_v,
              rel_bias_table, norm_mix, norm_ffn, w_ada, b_ada,
              w_in_a, q_gain_a, k_gain_a, sinks_a, w_out_a,
              w_in_b, q_gain_b, k_gain_b, lam_q1, lam_k1, lam_q2, lam_k2, sub_gain_b, w_out_b,
              w_in_c, w_out_c,
              w_group, b_group, w_router, b_router, w_gate, w_up, w_down):
    moe_w = (w_group, b_group, w_router, b_router, w_gate, w_up, w_down)

    def mix_prompt(l, h):
        i, kind = l // N_MIXERS, l % N_MIXERS
        if kind == 0:
            return mixer_a_prompt(h, w_in_a[i], q_gain_a[i], k_gain_a[i], sinks_a[i], w_out_a[i], rel_bias_table)
        if kind == 1:
            return mixer_b_prompt(h, w_in_b[i], q_gain_b[i], k_gain_b[i], lam_q1[i], lam_k1[i], lam_q2[i],
                                  lam_k2[i], sub_gain_b[i], w_out_b[i], rel_bias_table, lambda_init(l))
        return mixer_c_prompt(h, w_in_c[i], w_out_c[i])

    def mix_sample(l, h):
        i, kind = l // N_MIXERS, l % N_MIXERS
        if kind == 0:
            return mixer_a_sample(h, cache_a_k[i], cache_a_v[i], w_in_a[i], q_gain_a[i], k_gain_a[i],
                                  sinks_a[i], w_out_a[i], rel_bias_table)
        if kind == 1:
            return mixer_b_sample(h, cache_b_k[i], cache_b_v[i], w_in_b[i], q_gain_b[i], k_gain_b[i],
                                  lam_q1[i], lam_k1[i], lam_q2[i], lam_k2[i], sub_gain_b[i], w_out_b[i],
                                  rel_bias_table, lambda_init(l))
        return mixer_c_sample(h, cache_c_k[i], cache_c_v[i], w_in_c[i], w_out_c[i])

    y_prompt, st_p = trunk(x_prompt, c_prompt, norm_mix, norm_ffn, w_ada, b_ada, moe_w, mix_prompt)
    y_sample, st_s = trunk(x_sample, c_sample, norm_mix, norm_ffn, w_ada, b_ada, moe_w, mix_sample)
    return (y_prompt, y_sample,
            collect(st_p, 0, 0), collect(st_p, 0, 1),
            collect(st_p, 1, 0), collect(st_p, 1, 1),
            collect(st_p, 2, 0), collect(st_p, 2, 1),
            collect(st_s, 0, 0), collect(st_s, 0, 1),
            collect(st_s, 1, 0), collect(st_s, 1, 1),
            collect(st_s, 2, 0), collect(st_s, 2, 1))
```

```python
import functools
import math

import numpy as np
import jax
import jax.numpy as jnp
from jax import lax
from jax.experimental import pallas as pl
from jax.experimental.pallas import tpu as pltpu

F32 = jnp.float32
BF16 = jnp.bfloat16

D_MODEL = 2048
DEPTH = 4
CHUNK = 64
HEAD_DIM = 128
N_MIXERS = 3
A_HEADS = 16
A_KV_HEADS = 4
A_GROUP = A_HEADS // A_KV_HEADS
WINDOW_CHUNKS = 2
BAND = (WINDOW_CHUNKS + 1) * CHUNK
B_HEADS = 8
B_VDIM = 2 * HEAD_DIM
C_HEADS = 16
N_BUCKETS = 32
MAX_DISTANCE = 128
N_GROUPS = 4
EXPERTS_PER_GROUP = 4
N_EXPERTS = N_GROUPS * EXPERTS_PER_GROUP
D_EXPERT = D_MODEL // 4
EPS = 1e-6
QK_SCALE = HEAD_DIM ** -0.5

LANES = 128
MASKED = -1e30
VMEM_LIMIT = 56 * 1024 * 1024

NT_DIMS = (((1,), (1,)), ((), ()))


def _cparams(*sem):
    return pltpu.CompilerParams(dimension_semantics=sem, vmem_limit_bytes=VMEM_LIMIT)


def _lambda_init(layer):
    return 0.8 - 0.6 * math.exp(-0.3 * layer)


def _t5_bucket_np(rel):
    half = N_BUCKETS // 2
    max_exact = half // 2
    n = np.abs(rel)
    nf = np.maximum(n, 1).astype(np.float32)
    large = max_exact + (np.log(nf / np.float32(max_exact)) / np.float32(math.log(MAX_DISTANCE / max_exact))
                         * np.float32(half - max_exact)).astype(np.int32)
    large = np.minimum(large, half - 1)
    return np.where(rel > 0, half, 0) + np.where(n < max_exact, n, large)


def _bias_tile(table, q_pos, k_pos, valid):
    b = _t5_bucket_np(k_pos[None, :] - q_pos[:, None])
    t = jnp.moveaxis(table.astype(F32)[jnp.asarray(b)], -1, 0)
    return jnp.where(jnp.asarray(valid)[None], t, MASKED)


def _mod_spec(mod, tm, tn, imap_rows, imap_one):
    if mod.shape[1] == 1:
        return pl.BlockSpec((None, 1, tn), imap_one)
    return pl.BlockSpec((None, tm, tn), imap_rows)


def _ada_kernel(c_ref, w_ref, b_ref, o_ref):
    c = c_ref[...]
    a = (c / (1.0 + jnp.exp(-c))).astype(BF16)
    o_ref[...] = jnp.dot(a, w_ref[...].astype(BF16), preferred_element_type=F32) + b_ref[...]


def _ada_mod(c_all, w_ada, b_ada):
    rows = c_all.shape[0]
    depth, d, n = w_ada.shape
    tn = 1024
    return pl.pallas_call(
        _ada_kernel,
        grid=(depth, n // tn),
        in_specs=[pl.BlockSpec((rows, d), lambda l, j: (0, 0)),
                  pl.BlockSpec((None, d, tn), lambda l, j: (l, 0, j)),
                  pl.BlockSpec((None, 1, tn), lambda l, j: (l, 0, j))],
        out_specs=pl.BlockSpec((None, rows, tn), lambda l, j: (l, 0, j)),
        out_shape=jax.ShapeDtypeStruct((depth, rows, n), F32),
        compiler_params=_cparams("parallel", "parallel"),
    )(c_all, w_ada, b_ada.reshape(depth, 1, n))


def _norm_mod_f32(x, g, sc, sh):
    y = x * lax.rsqrt(jnp.mean(x * x, axis=-1, keepdims=True) + EPS) * g
    return y * (1.0 + sc) + sh


def _norm_mod_kernel(x_ref, g_ref, sc_ref, sh_ref, o_ref):
    o_ref[...] = _norm_mod_f32(x_ref[...], g_ref[...], sc_ref[...], sh_ref[...]).astype(o_ref.dtype)


def _norm_mod(x, gain, sc, sh):
    b, s, d = x.shape
    tm = min(512, s)
    rows = lambda bb, i: (bb, i, 0)
    one = lambda bb, i: (bb, 0, 0)
    return pl.pallas_call(
        _norm_mod_kernel,
        grid=(b, s // tm),
        in_specs=[pl.BlockSpec((None, tm, d), rows),
                  pl.BlockSpec((1, d), lambda bb, i: (0, 0)),
                  _mod_spec(sc, tm, d, rows, one),
                  _mod_spec(sh, tm, d, rows, one)],
        out_specs=pl.BlockSpec((None, tm, d), rows),
        out_shape=jax.ShapeDtypeStruct((b, s, d), BF16),
        compiler_params=_cparams("parallel", "parallel"),
    )(x, gain.reshape(1, d), sc, sh)


def _proj_kernel(*refs, has_gain, scale, out32, outbf):
    x_ref, w_ref = refs[0], refs[1]
    pos = 2
    g_ref = None
    if has_gain:
        g_ref = refs[pos]
        pos += 1
    o32_ref = obf_ref = None
    if out32:
        o32_ref = refs[pos]
        pos += 1
    if outbf:
        obf_ref = refs[pos]
    acc = jnp.dot(x_ref[...], w_ref[...], preferred_element_type=F32)
    tn = acc.shape[1]
    for c in range(tn // HEAD_DIM):
        sl = slice(c * HEAD_DIM, (c + 1) * HEAD_DIM)
        t = acc[:, sl]
        if has_gain:
            t = t * lax.rsqrt(jnp.mean(t * t, axis=-1, keepdims=True) + EPS) * g_ref[...]
        if out32:
            o32_ref[:, sl] = t
        if outbf:
            obf_ref[:, sl] = (t * scale).astype(BF16) if scale != 1.0 else t.astype(BF16)


def _proj(x, w, gain=None, scale=1.0, out32=False, outbf=True):
    b, s, k = x.shape
    n = w.shape[1]
    tm = min(1024, s)
    tn = min(1024, n)
    in_specs = [pl.BlockSpec((None, tm, k), lambda j, bb, i: (bb, i, 0)),
                pl.BlockSpec((k, tn), lambda j, bb, i: (0, j))]
    args = [x, w]
    if gain is not None:
        in_specs.append(pl.BlockSpec((1, HEAD_DIM), lambda j, bb, i: (0, 0)))
        args.append(gain.reshape(1, HEAD_DIM).astype(F32))
    out_specs, out_shape = [], []
    for flag, dt in ((out32, F32), (outbf, BF16)):
        if flag:
            out_specs.append(pl.BlockSpec((None, tm, tn), lambda j, bb, i: (bb, i, j)))
            out_shape.append(jax.ShapeDtypeStruct((b, s, n), dt))
    outs = pl.pallas_call(
        functools.partial(_proj_kernel, has_gain=gain is not None, scale=scale, out32=out32, outbf=outbf),
        grid=(n // tn, b, s // tm),
        in_specs=in_specs, out_specs=out_specs, out_shape=out_shape,
        compiler_params=_cparams("parallel", "parallel", "parallel"),
    )(*args)
    return outs if len(outs) > 1 else outs[0]


def _out_res_kernel(o_ref, w_ref, x_ref, g_ref, y_ref):
    acc = jnp.dot(o_ref[...], w_ref[...], preferred_element_type=F32)
    y_ref[...] = x_ref[...] + g_ref[...] * acc


def _out_res(o, w, x, gate):
    b, s, k = o.shape
    n = w.shape[1]
    tm = min(1024, s)
    tn = min(1024, n)
    rows = lambda j, bb, i: (bb, i, j)
    one = lambda j, bb, i: (bb, 0, j)
    return pl.pallas_call(
        _out_res_kernel,
        grid=(n // tn, b, s // tm),
        in_specs=[pl.BlockSpec((None, tm, k), lambda j, bb, i: (bb, i, 0)),
                  pl.BlockSpec((k, tn), lambda j, bb, i: (0, j)),
                  pl.BlockSpec((None, tm, tn), rows),
                  _mod_spec(gate, tm, tn, rows, one)],
        out_specs=pl.BlockSpec((None, tm, tn), rows),
        out_shape=jax.ShapeDtypeStruct((b, s, n), F32),
        compiler_params=_cparams("parallel", "parallel", "parallel"),
    )(o, w, x, gate)


def _sink_attend(q, k_parts, v_parts, bias_parts, sink):
    s_parts = [lax.dot_general(q, kp, NT_DIMS, preferred_element_type=F32) + bp
               for kp, bp in zip(k_parts, bias_parts)]
    m = sink
    for sp in s_parts:
        m = jnp.maximum(m, jnp.max(sp, axis=-1, keepdims=True))
    denom = jnp.exp(sink - m)
    e_parts = []
    for sp in s_parts:
        e = jnp.exp(sp - m)
        denom = denom + jnp.sum(e, axis=-1, keepdims=True)
        e_parts.append(e)
    inv = 1.0 / denom
    o = None
    for e, vp in zip(e_parts, v_parts):
        t = jnp.dot((e * inv).astype(BF16), vp, preferred_element_type=F32)
        o = t if o is None else o + t
    return o


def _attn_a_prompt_kernel(q_ref, kp_ref, k_ref, vp_ref, v_ref, bias_ref, bias0_ref, bias1_ref, sink_ref, o_ref,
                          *, tq):
    i = pl.program_id(2)
    kcat = jnp.concatenate([kp_ref[...], k_ref[...]], axis=0)
    vcat = jnp.concatenate([vp_ref[...], v_ref[...]], axis=0)
    sink = sink_ref[...]
    first = i == 0
    for c in range(tq // CHUNK):
        rs = slice(c * CHUNK, (c + 1) * CHUNK)
        qc = jnp.concatenate([q_ref[rs, g * HEAD_DIM:(g + 1) * HEAD_DIM] for g in range(A_GROUP)], axis=0)
        kb = kcat[c * CHUNK:c * CHUNK + BAND]
        vb = vcat[c * CHUNK:c * CHUNK + BAND]
        bias = bias_ref[...]
        if c == 0:
            bias = jnp.where(first, bias0_ref[...], bias)
        elif c == 1:
            bias = jnp.where(first, bias1_ref[...], bias)
        o = _sink_attend(qc, [kb], [vb], [bias], sink)
        for g in range(A_GROUP):
            o_ref[rs, g * HEAD_DIM:(g + 1) * HEAD_DIM] = o[g * CHUNK:(g + 1) * CHUNK].astype(o_ref.dtype)


def _attn_a_prompt(q, k, v, table, sinks):
    b, s, _ = q.shape
    tq = min(512, s)
    prev = WINDOW_CHUNKS * CHUNK
    r = tq // prev
    k_loc = np.arange(BAND)
    q_loc = prev + np.arange(CHUNK)

    def tiles(first_key):
        valid = np.broadcast_to((k_loc >= first_key)[None, :], (CHUNK, BAND))
        t = _bias_tile(table, q_loc, k_loc, valid)
        return t.reshape(A_KV_HEADS, A_GROUP * CHUNK, BAND)

    bias, bias0, bias1 = tiles(0), tiles(prev), tiles(CHUNK)
    sink_col = jnp.repeat(sinks.astype(F32).reshape(A_KV_HEADS, A_GROUP), CHUNK, axis=1)[..., None]
    qmap = lambda bb, h, i: (bb, i, h)
    pmap = lambda bb, h, i: (bb, jnp.maximum(i * r - 1, 0), h)
    bmap = lambda bb, h, i: (h, 0, 0)
    bspec = pl.BlockSpec((None, A_GROUP * CHUNK, BAND), bmap)
    return pl.pallas_call(
        functools.partial(_attn_a_prompt_kernel, tq=tq),
        grid=(b, A_KV_HEADS, s // tq),
        in_specs=[pl.BlockSpec((None, tq, A_GROUP * HEAD_DIM), qmap),
                  pl.BlockSpec((None, prev, HEAD_DIM), pmap),
                  pl.BlockSpec((None, tq, HEAD_DIM), qmap),
                  pl.BlockSpec((None, prev, HEAD_DIM), pmap),
                  pl.BlockSpec((None, tq, HEAD_DIM), qmap),
                  bspec, bspec, bspec,
                  pl.BlockSpec((None, A_GROUP * CHUNK, 1), bmap)],
        out_specs=pl.BlockSpec((None, tq, A_GROUP * HEAD_DIM), qmap),
        out_shape=jax.ShapeDtypeStruct((b, s, D_MODEL), BF16),
        compiler_params=_cparams("parallel", "parallel", "parallel"),
    )(q, k, k, v, v, bias, bias0, bias1, sink_col)


def _attn_a_sample_kernel(q_ref, kc_ref, kn_ref, vc_ref, vn_ref, bc_ref, bn_ref, sink_ref, o_ref, *, t):
    qs = jnp.concatenate([q_ref[:, g * HEAD_DIM:(g + 1) * HEAD_DIM] for g in range(A_GROUP)], axis=0)
    o = _sink_attend(qs,
                     [kc_ref[...].astype(BF16), kn_ref[...].astype(BF16)],
                     [vc_ref[...].astype(BF16), vn_ref[...].astype(BF16)],
                     [bc_ref[...], bn_ref[...]], sink_ref[...])
    for g in range(A_GROUP):
        o_ref[:, g * HEAD_DIM:(g + 1) * HEAD_DIM] = o[g * t:(g + 1) * t].astype(o_ref.dtype)


def _attn_a_sample(q, ck, cv, kn, vn, table, sinks, past_len):
    b, t, _ = q.shape
    cache = ck.shape[1]
    q_pos = past_len + np.arange(t)
    k_pos = np.concatenate([past_len - cache + np.arange(cache), q_pos])
    qc, kc = q_pos[:, None] // CHUNK, k_pos[None, :] // CHUNK
    valid = (kc <= qc) & (kc >= qc - WINDOW_CHUNKS)
    bias = _bias_tile(table, q_pos, k_pos, valid).reshape(A_KV_HEADS, A_GROUP * t, cache + t)
    sink_col = jnp.repeat(sinks.astype(F32).reshape(A_KV_HEADS, A_GROUP), t, axis=1)[..., None]
    hmap = lambda bb, h: (bb, 0, h)
    bmap = lambda bb, h: (h, 0, 0)
    return pl.pallas_call(
        functools.partial(_attn_a_sample_kernel, t=t),
        grid=(b, A_KV_HEADS),
        in_specs=[pl.BlockSpec((None, t, A_GROUP * HEAD_DIM), hmap),
                  pl.BlockSpec((None, cache, HEAD_DIM), hmap),
                  pl.BlockSpec((None, t, HEAD_DIM), hmap),
                  pl.BlockSpec((None, cache, HEAD_DIM), hmap),
                  pl.BlockSpec((None, t, HEAD_DIM), hmap),
                  pl.BlockSpec((None, A_GROUP * t, cache), bmap),
                  pl.BlockSpec((None, A_GROUP * t, t), bmap),
                  pl.BlockSpec((None, A_GROUP * t, 1), bmap)],
        out_specs=pl.BlockSpec((None, t, A_GROUP * HEAD_DIM), hmap),
        out_shape=jax.ShapeDtypeStruct((b, t, D_MODEL), BF16),
        compiler_params=_cparams("parallel", "parallel"),
    )(q, ck, kn, cv, vn, bias[..., :cache], bias[..., cache:], sink_col)


def _diff_lambda(lam_ref, lam_init):
    lp = lam_ref[...]
    a = jnp.sum(lp[0:1] * lp[1:2], axis=-1, keepdims=True)
    c = jnp.sum(lp[2:3] * lp[3:4], axis=-1, keepdims=True)
    return jnp.exp(a) - jnp.exp(c) + lam_init


def _finish_b(o0, o1, lam, gsub, lam_init):
    o = o0 - lam * o1
    o = o * lax.rsqrt(jnp.mean(o * o, axis=-1, keepdims=True) + EPS) * gsub
    return o * (1.0 - lam_init)


def _attn_b_prompt_kernel(q_ref, k_ref, v_ref, bias_ref, far_ref, lam_ref, gsub_ref, o_ref,
                          m_sc, l_sc, acc_sc, *, t, lam_init):
    i = pl.program_id(2)
    m_sc[...] = jnp.full(m_sc.shape, MASKED, F32)
    l_sc[...] = jnp.zeros(l_sc.shape, F32)
    acc_sc[...] = jnp.zeros(acc_sc.shape, F32)

    def step(j, bias_fn):
        kb = k_ref[pl.ds(pl.multiple_of(j * t, t), t), :]
        vb = v_ref[pl.ds(pl.multiple_of(j * t, t), t), :]
        for mp in range(2):
            sl = slice(mp * HEAD_DIM, (mp + 1) * HEAD_DIM)
            s = lax.dot_general(q_ref[:, sl], kb[:, sl], NT_DIMS, preferred_element_type=F32)
            s = bias_fn(s, mp)
            m_old = m_sc[mp]
            m_new = jnp.maximum(m_old, jnp.max(s, axis=-1, keepdims=True))
            alpha = jnp.exp(m_old - m_new)
            p = jnp.exp(s - m_new)
            l_sc[mp] = alpha * l_sc[mp] + jnp.sum(p, axis=-1, keepdims=True)
            acc_sc[mp] = alpha * acc_sc[mp] + jnp.dot(p.astype(BF16), vb, preferred_element_type=F32)
            m_sc[mp] = m_new

    def far_body(j, carry):
        step(j, lambda s, mp: s + far_ref[mp][:, 0:1])
        return carry

    lax.fori_loop(0, jnp.maximum(i - 1, 0), far_body, 0)

    @pl.when(i >= 1)
    def _():
        step(i - 1, lambda s, mp: s + bias_ref[mp, 1])

    step(i, lambda s, mp: s + bias_ref[mp, 0])
    lam = _diff_lambda(lam_ref, lam_init)
    o = _finish_b(acc_sc[0] / l_sc[0], acc_sc[1] / l_sc[1], lam, gsub_ref[...], lam_init)
    o_ref[...] = o.astype(o_ref.dtype)


def _attn_b_prompt(q, k, v, table, lam_rows, gsub, lam_init):
    b, s, _ = q.shape
    t = min(512, s)
    loc = np.arange(t)
    chunk_ok = (loc[None, :] // CHUNK) <= (loc[:, None] // CHUNK)
    diag = _bias_tile(table, loc, loc, chunk_ok)
    sub = _bias_tile(table, t + loc, loc, np.ones((t, t), bool))
    bias = jnp.stack([diag, sub], axis=1).reshape(B_HEADS, 2, 2, t, t)
    far_rel = -(t + 1 + np.arange(max(s - t, 1)))
    far_bucket = _t5_bucket_np(far_rel)
    assert np.all(far_bucket == far_bucket[0])
    far = jnp.broadcast_to(table.astype(F32)[int(far_bucket[0])].reshape(B_HEADS, 2, 1, 1),
                           (B_HEADS, 2, 1, LANES))
    qmap = lambda bb, h, i: (bb, i, h)
    kmap = lambda bb, h, i: (bb, 0, h)
    return pl.pallas_call(
        functools.partial(_attn_b_prompt_kernel, t=t, lam_init=lam_init),
        grid=(b, B_HEADS, s // t),
        in_specs=[pl.BlockSpec((None, t, B_VDIM), qmap),
                  pl.BlockSpec((None, s, B_VDIM), kmap),
                  pl.BlockSpec((None, s, B_VDIM), kmap),
                  pl.BlockSpec((None, 2, 2, t, t), lambda bb, h, i: (h, 0, 0, 0, 0)),
                  pl.BlockSpec((None, 2, 1, LANES), lambda bb, h, i: (h, 0, 0, 0)),
                  pl.BlockSpec((4, HEAD_DIM), lambda bb, h, i: (0, 0)),
                  pl.BlockSpec((1, B_VDIM), lambda bb, h, i: (0, 0))],
        out_specs=pl.BlockSpec((None, t, B_VDIM), qmap),
        out_shape=jax.ShapeDtypeStruct((b, s, D_MODEL), BF16),
        scratch_shapes=[pltpu.VMEM((2, t, 1), F32), pltpu.VMEM((2, t, 1), F32),
                        pltpu.VMEM((2, t, B_VDIM), F32)],
        compiler_params=_cparams("parallel", "parallel", "parallel"),
    )(q, k, v, bias, far, lam_rows, gsub.reshape(1, B_VDIM).astype(F32))


def _attn_b_sample_kernel(q_ref, kc_ref, kn_ref, vc_ref, vn_ref, bc_ref, bn_ref, lam_ref, gsub_ref, o_ref,
                          *, lam_init):
    kc = kc_ref[...].astype(BF16)
    kn = kn_ref[...].astype(BF16)
    vc = vc_ref[...].astype(BF16)
    vn = vn_ref[...].astype(BF16)
    outs = []
    for mp in range(2):
        sl = slice(mp * HEAD_DIM, (mp + 1) * HEAD_DIM)
        qm = q_ref[:, sl]
        sc = lax.dot_general(qm, kc[:, sl], NT_DIMS, preferred_element_type=F32) + bc_ref[mp]
        sn = lax.dot_general(qm, kn[:, sl], NT_DIMS, preferred_element_type=F32) + bn_ref[mp]
        m = jnp.maximum(jnp.max(sc, axis=-1, keepdims=True), jnp.max(sn, axis=-1, keepdims=True))
        pc = jnp.exp(sc - m)
        pn = jnp.exp(sn - m)
        inv = 1.0 / (jnp.sum(pc, axis=-1, keepdims=True) + jnp.sum(pn, axis=-1, keepdims=True))
        outs.append(jnp.dot((pc * inv).astype(BF16), vc, preferred_element_type=F32)
                    + jnp.dot((pn * inv).astype(BF16), vn, preferred_element_type=F32))
    lam = _diff_lambda(lam_ref, lam_init)
    o_ref[...] = _finish_b(outs[0], outs[1], lam, gsub_ref[...], lam_init).astype(o_ref.dtype)


def _attn_b_sample(q, ck, cv, kn, vn, table, lam_rows, gsub, lam_init, past_len):
    b, t, _ = q.shape
    past = ck.shape[1]
    q_pos = past_len + np.arange(t)
    k_pos = np.arange(past_len + t)
    valid = (k_pos[None, :] // CHUNK) <= (q_pos[:, None] // CHUNK)
    bias = _bias_tile(table, q_pos, k_pos, valid).reshape(B_HEADS, 2, t, past + t)
    hmap = lambda bb, h: (bb, 0, h)
    bmap = lambda bb, h: (h, 0, 0, 0)
    return pl.pallas_call(
        functools.partial(_attn_b_sample_kernel, lam_init=lam_init),
        grid=(b, B_HEADS),
        in_specs=[pl.BlockSpec((None, t, B_VDIM), hmap),
                  pl.BlockSpec((None, past, B_VDIM), hmap),
                  pl.BlockSpec((None, t, B_VDIM), hmap),
                  pl.BlockSpec((None, past, B_VDIM), hmap),
                  pl.BlockSpec((None, t, B_VDIM), hmap),
                  pl.BlockSpec((None, 2, t, past), bmap),
                  pl.BlockSpec((None, 2, t, t), bmap),
                  pl.BlockSpec((4, HEAD_DIM), lambda bb, h: (0, 0)),
                  pl.BlockSpec((1, B_VDIM), lambda bb, h: (0, 0))],
        out_specs=pl.BlockSpec((None, t, B_VDIM), hmap),
        out_shape=jax.ShapeDtypeStruct((b, t, D_MODEL), BF16),
        compiler_params=_cparams("parallel", "parallel"),
    )(q, ck, kn, cv, vn, bias[..., :past], bias[..., past:], lam_rows, gsub.reshape(1, B_VDIM).astype(F32))


def _strict_upper_ones(n):
    idx = np.arange(n)
    return jnp.asarray((idx[:, None] > idx[None, :]).astype(np.float32), dtype=BF16)


def _sb_block(q, kb, vb, u, run, mask):
    z = lax.dot_general(q, kb, NT_DIMS, preferred_element_type=F32)
    soft = jnp.log(1.0 + jnp.exp(-jnp.abs(z)))
    log_1m = -(jnp.maximum(z, 0.0) + soft)
    log_sig = jnp.minimum(z, 0.0) - soft
    if mask is not None:
        log_1m = jnp.where(mask, log_1m, 0.0)
    after = jnp.dot(log_1m.astype(BF16), u, preferred_element_type=F32) + run
    a = jnp.exp(log_sig + after)
    if mask is not None:
        a = jnp.where(mask, a, 0.0)
    o = jnp.dot(a.astype(BF16), vb, preferred_element_type=F32)
    return o, run + jnp.sum(log_1m, axis=-1, keepdims=True)


def _attn_c_prompt_kernel(q_ref, k_ref, v_ref, u_ref, o_ref, run_sc, acc_sc, *, t):
    i = pl.program_id(2)
    q = q_ref[...]
    u = u_ref[...]
    row = lax.broadcasted_iota(jnp.int32, (t, t), 0)
    col = lax.broadcasted_iota(jnp.int32, (t, t), 1)
    start = pl.multiple_of(i * t, t)
    o, run = _sb_block(q, k_ref[pl.ds(start, t), :], v_ref[pl.ds(start, t), :], u,
                       jnp.zeros((t, 1), F32), col < row)
    acc_sc[...] = o
    run_sc[...] = run

    def body(n, carry):
        st = pl.multiple_of((i - 1 - n) * t, t)
        o, run = _sb_block(q, k_ref[pl.ds(st, t), :], v_ref[pl.ds(st, t), :], u, run_sc[...], None)
        acc_sc[...] += o
        run_sc[...] = run
        return carry

    lax.fori_loop(0, i, body, 0)
    o_ref[...] = acc_sc[...].astype(o_ref.dtype)


def _attn_c_prompt(q, k, v):
    b, s, _ = q.shape
    t = min(256, s)
    qmap = lambda bb, h, i: (bb, i, h)
    kmap = lambda bb, h, i: (bb, 0, h)
    return pl.pallas_call(
        functools.partial(_attn_c_prompt_kernel, t=t),
        grid=(b, C_HEADS, s // t),
        in_specs=[pl.BlockSpec((None, t, HEAD_DIM), qmap),
                  pl.BlockSpec((None, s, HEAD_DIM), kmap),
                  pl.BlockSpec((None, s, HEAD_DIM), kmap),
                  pl.BlockSpec((t, t), lambda bb, h, i: (0, 0))],
        out_specs=pl.BlockSpec((None, t, HEAD_DIM), qmap),
        out_shape=jax.ShapeDtypeStruct((b, s, D_MODEL), BF16),
        scratch_shapes=[pltpu.VMEM((t, 1), F32), pltpu.VMEM((t, HEAD_DIM), F32)],
        compiler_params=_cparams("parallel", "parallel", "parallel"),
    )(q, k, v, _strict_upper_ones(t))


def _attn_c_sample_kernel(q_ref, kc_ref, kn_ref, vc_ref, vn_ref, u_ref, o_ref, *, t, past, tk):
    q = q_ref[...]
    u = u_ref[...]
    row = lax.broadcasted_iota(jnp.int32, (t, t), 0)
    col = lax.broadcasted_iota(jnp.int32, (t, t), 1)
    acc, run = _sb_block(q, kn_ref[...].astype(BF16), vn_ref[...].astype(BF16), u[:t, :t],
                         jnp.zeros((t, 1), F32), col < row)
    for j in reversed(range(past // tk)):
        rs = slice(j * tk, (j + 1) * tk)
        o, run = _sb_block(q, kc_ref[rs, :].astype(BF16), vc_ref[rs, :].astype(BF16), u, run, None)
        acc = acc + o
    o_ref[...] = acc.astype(o_ref.dtype)


def _attn_c_sample(q, ck, cv, kn, vn):
    b, t, _ = q.shape
    past = ck.shape[1]
    tk = min(256, past)
    hmap = lambda bb, h: (bb, 0, h)
    return pl.pallas_call(
        functools.partial(_attn_c_sample_kernel, t=t, past=past, tk=tk),
        grid=(b, C_HEADS),
        in_specs=[pl.BlockSpec((None, t, HEAD_DIM), hmap),
                  pl.BlockSpec((None, past, HEAD_DIM), hmap),
                  pl.BlockSpec((None, t, HEAD_DIM), hmap),
                  pl.BlockSpec((None, past, HEAD_DIM), hmap),
                  pl.BlockSpec((None, t, HEAD_DIM), hmap),
                  pl.BlockSpec((tk, tk), lambda bb, h: (0, 0))],
        out_specs=pl.BlockSpec((None, t, HEAD_DIM), hmap),
        out_shape=jax.ShapeDtypeStruct((b, t, D_MODEL), BF16),
        compiler_params=_cparams("parallel", "parallel"),
    )(q, ck, kn, cv, vn, _strict_upper_ones(tk))


ROUTE_LANES = LANES
EXPERT_LANE0 = N_GROUPS
ROW_LANES = LANES
ROW_TILE = D_MODEL // ROW_LANES


def _store_row_tiled(ref, val):
    rows = val.shape[0]
    for s in range(ROW_TILE):
        ref[pl.ds(s, rows, stride=ROW_TILE), :] = val[:, s * ROW_LANES:(s + 1) * ROW_LANES]


def _load_row_tiled(ref, rows, s, base=0):
    return ref[pl.ds(base * ROW_TILE + s, rows, stride=ROW_TILE), :]


def _router_kernel(x_ref, g_ref, sc_ref, sh_ref, w_ref, b_ref, h_ref, r_ref):
    h = _norm_mod_f32(x_ref[...], g_ref[...], sc_ref[...], sh_ref[...])
    _store_row_tiled(h_ref, h)
    h_hi = h.astype(BF16)
    h_lo = (h - h_hi.astype(F32)).astype(BF16)
    w = w_ref[...]
    both = jnp.dot(h_hi, w, preferred_element_type=F32)
    logits = (both[:, :ROUTE_LANES] + both[:, ROUTE_LANES:]
              + jnp.dot(h_lo, w[:, :ROUTE_LANES], preferred_element_type=F32) + b_ref[...])
    lane = lax.broadcasted_iota(jnp.int32, logits.shape, 1)
    lanef = lane.astype(F32)

    def first_lane_of_max(v, vmax):
        return jnp.min(jnp.where(v == vmax, lanef, float(ROUTE_LANES)), axis=-1, keepdims=True)

    lg = jnp.where(lane < N_GROUPS, logits, MASKED)
    mg = jnp.max(lg, axis=-1, keepdims=True)
    gate = 1.0 / jnp.sum(jnp.exp(lg - mg), axis=-1, keepdims=True)
    gi = first_lane_of_max(lg, mg)
    lo = EXPERT_LANE0 + EXPERTS_PER_GROUP * gi
    le = jnp.where((lanef >= lo) & (lanef < lo + EXPERTS_PER_GROUP), logits, MASKED)
    v1 = jnp.max(le, axis=-1, keepdims=True)
    i1 = first_lane_of_max(le, v1)
    le2 = jnp.where(lanef == i1, MASKED, le)
    v2 = jnp.max(le2, axis=-1, keepdims=True)
    i2 = first_lane_of_max(le2, v2)
    e21 = jnp.exp(v2 - v1)
    w1 = gate / (1.0 + e21)
    w2 = w1 * e21
    r_ref[...] = jnp.where(lane == 0, i1 - EXPERT_LANE0,
                           jnp.where(lane == 1, i2 - EXPERT_LANE0,
                                     jnp.where(lane == 2, w1, jnp.where(lane == 3, w2, 0.0))))


def _router(x, gain, sc, sh, w_cat, b_row):
    b, s, d = x.shape
    tm = min(512, s)
    rows = lambda bb, i: (bb, i, 0)
    one = lambda bb, i: (bb, 0, 0)
    return pl.pallas_call(
        _router_kernel,
        grid=(b, s // tm),
        in_specs=[pl.BlockSpec((None, tm, d), rows),
                  pl.BlockSpec((1, d), lambda bb, i: (0, 0)),
                  _mod_spec(sc, tm, d, rows, one),
                  _mod_spec(sh, tm, d, rows, one),
                  pl.BlockSpec((d, 2 * ROUTE_LANES), lambda bb, i: (0, 0)),
                  pl.BlockSpec((1, ROUTE_LANES), lambda bb, i: (0, 0))],
        out_specs=[pl.BlockSpec((None, tm * ROW_TILE, ROW_LANES), rows),
                   pl.BlockSpec((None, tm, ROUTE_LANES), rows)],
        out_shape=[jax.ShapeDtypeStruct((b, s * ROW_TILE, ROW_LANES), F32),
                   jax.ShapeDtypeStruct((b, s, ROUTE_LANES), F32)],
        compiler_params=_cparams("parallel", "parallel"),
    )(x, gain.reshape(1, d), sc, sh, w_cat, b_row)


def _router_weights(w_group, b_group, w_router, b_router):
    w = jnp.concatenate([w_group, jnp.moveaxis(w_router, 0, 1).reshape(D_MODEL, N_EXPERTS)], axis=1)
    w = jnp.pad(w.astype(F32), ((0, 0), (0, ROUTE_LANES - w.shape[1])))
    hi = w.astype(BF16)
    lo = (w - hi.astype(F32)).astype(BF16)
    bias = jnp.concatenate([b_group, b_router.reshape(N_EXPERTS)]).astype(F32)
    bias = jnp.pad(bias, (0, ROUTE_LANES - bias.shape[0])).reshape(1, ROUTE_LANES)
    return jnp.concatenate([hi, lo], axis=1), bias


def _row_gather(src_hbm, idx_ref, n_rows, dst, sem):
    def body(r, carry):
        src = pl.multiple_of(idx_ref[0, r] * ROW_TILE, ROW_TILE)
        pltpu.make_async_copy(src_hbm.at[pl.ds(src, ROW_TILE)],
                              dst.at[pl.ds(pl.multiple_of(r * ROW_TILE, ROW_TILE), ROW_TILE)], sem).start()
        return carry
    lax.fori_loop(0, n_rows, body, 0, unroll=8)


def _row_gather_wait(src_hbm, n_rows, dst, sem):
    pltpu.make_async_copy(src_hbm.at[pl.ds(0, n_rows * ROW_TILE)], dst, sem).wait()


def _expert_kernel(te_ref, idx_ref, idx_next_ref, h_hbm, wgu_ref, wd_ref, rw_ref, y_ref, buf, sem, *, tm):
    i = pl.program_id(0)
    n = pl.num_programs(0)
    slot = i % 2

    @pl.when(i == 0)
    def _():
        _row_gather(h_hbm, idx_ref, tm, buf.at[0], sem.at[0])

    @pl.when(i + 1 < n)
    def _():
        _row_gather(h_hbm, idx_next_ref, tm, buf.at[1 - slot], sem.at[1 - slot])

    _row_gather_wait(h_hbm, tm, buf.at[slot], sem.at[slot])
    xb = buf.at[slot]
    x = jnp.concatenate([_load_row_tiled(xb, tm, s).astype(BF16) for s in range(ROW_TILE)], axis=1)
    gu = jnp.dot(x, wgu_ref[...], preferred_element_type=F32)
    gate, up = gu[:, :D_EXPERT], gu[:, D_EXPERT:]
    hid = (gate / (1.0 + jnp.exp(-gate)) * up).astype(BF16)
    y = jnp.dot(hid, wd_ref[...], preferred_element_type=F32) * rw_ref[...]
    _store_row_tiled(y_ref, y)


def _experts(h_rt, tile_expert, src_rows, row_w, w_gu, w_down, tm):
    n_tiles = tile_expert.shape[0]
    d = D_MODEL
    grid_spec = pltpu.PrefetchScalarGridSpec(
        num_scalar_prefetch=1,
        grid=(n_tiles,),
        in_specs=[pl.BlockSpec((None, 1, tm), lambda i, te: (i, 0, 0), memory_space=pltpu.SMEM),
                  pl.BlockSpec((None, 1, tm), lambda i, te: (jnp.minimum(i + 1, n_tiles - 1), 0, 0),
                               memory_space=pltpu.SMEM),
                  pl.BlockSpec(memory_space=pl.ANY),
                  pl.BlockSpec((None, d, 2 * D_EXPERT), lambda i, te: (te[i], 0, 0)),
                  pl.BlockSpec((None, D_EXPERT, d), lambda i, te: (te[i], 0, 0)),
                  pl.BlockSpec((tm, 1), lambda i, te: (i, 0))],
        out_specs=pl.BlockSpec((tm * ROW_TILE, ROW_LANES), lambda i, te: (i, 0)),
        scratch_shapes=[pltpu.VMEM((2, tm * ROW_TILE, ROW_LANES), F32), pltpu.SemaphoreType.DMA((2,))],
    )
    idx3 = src_rows.reshape(n_tiles, 1, tm)
    return pl.pallas_call(
        functools.partial(_expert_kernel, tm=tm),
        grid_spec=grid_spec,
        out_shape=jax.ShapeDtypeStruct((n_tiles * tm * ROW_TILE, ROW_LANES), F32),
        compiler_params=_cparams("arbitrary"),
    )(tile_expert, idx3, idx3, h_rt, w_gu, w_down, row_w)


def _combine_kernel(pos_ref, pos_next_ref, y_hbm, x_ref, g_ref, o_ref, buf, sem, *, tc):
    i = pl.program_id(0) * pl.num_programs(1) + pl.program_id(1)
    n = pl.num_programs(0) * pl.num_programs(1)
    slot = i % 2

    @pl.when(i == 0)
    def _():
        _row_gather(y_hbm, pos_ref, 2 * tc, buf.at[0], sem.at[0])

    @pl.when(i + 1 < n)
    def _():
        _row_gather(y_hbm, pos_next_ref, 2 * tc, buf.at[1 - slot], sem.at[1 - slot])

    _row_gather_wait(y_hbm, 2 * tc, buf.at[slot], sem.at[slot])
    yb = buf.at[slot]
    for s in range(ROW_TILE):
        sl = slice(s * ROW_LANES, (s + 1) * ROW_LANES)
        y = _load_row_tiled(yb, tc, s) + _load_row_tiled(yb, tc, s, base=tc)
        o_ref[:, sl] = x_ref[:, sl] + g_ref[:, sl] * y


def _combine(x, y_rt, pos, gate, tc):
    b, s, d = x.shape
    nt = s // tc
    n = b * nt
    pos3 = pos.reshape(n, tc, 2).transpose(0, 2, 1).reshape(n, 1, 2 * tc)
    rows = lambda bb, i: (bb, i, 0)
    one = lambda bb, i: (bb, 0, 0)
    return pl.pallas_call(
        functools.partial(_combine_kernel, tc=tc),
        grid=(b, nt),
        in_specs=[pl.BlockSpec((None, 1, 2 * tc), lambda bb, i: (bb * nt + i, 0, 0), memory_space=pltpu.SMEM),
                  pl.BlockSpec((None, 1, 2 * tc), lambda bb, i: (jnp.minimum(bb * nt + i + 1, n - 1), 0, 0),
                               memory_space=pltpu.SMEM),
                  pl.BlockSpec(memory_space=pl.ANY),
                  pl.BlockSpec((None, tc, d), rows),
                  _mod_spec(gate, tc, d, rows, one)],
        out_specs=pl.BlockSpec((None, tc, d), rows),
        out_shape=jax.ShapeDtypeStruct((b, s, d), F32),
        scratch_shapes=[pltpu.VMEM((2, 2 * tc * ROW_TILE, ROW_LANES), F32), pltpu.SemaphoreType.DMA((2,))],
        compiler_params=_cparams("arbitrary", "arbitrary"),
    )(pos3, pos3, y_rt, x, gate)


def _moe(x, gain, sc, sh, gate, wts):
    w_cat, b_row, w_gu, w_down = wts
    b, s, d = x.shape
    tokens = b * s
    h, route = _router(x, gain, sc, sh, w_cat, b_row)
    route = route.reshape(tokens, ROUTE_LANES)
    ids = route[:, 0:2].astype(jnp.int32).reshape(-1)
    wv = route[:, 2:4].reshape(-1)
    tm = 256 if tokens >= 4096 else 128
    n_assign = 2 * tokens
    n_tiles = (n_assign + N_EXPERTS * (tm - 1)) // tm + 1
    onehot = (ids[:, None] == jnp.arange(N_EXPERTS)[None, :]).astype(jnp.int32)
    csum = jnp.cumsum(onehot, axis=0)
    counts = csum[-1]
    rank = jnp.take_along_axis(csum, ids[:, None], axis=1)[:, 0] - 1
    padded = ((counts + tm - 1) // tm) * tm
    ends = jnp.cumsum(padded)
    pos = (ends - padded)[ids] + rank
    tile_expert = jnp.minimum(jnp.searchsorted(ends // tm, jnp.arange(n_tiles), side='right'),
                              N_EXPERTS - 1).astype(jnp.int32)
    src_rows = jnp.zeros((n_tiles * tm,), jnp.int32).at[pos].set(jnp.arange(n_assign, dtype=jnp.int32) // 2)
    row_w = jnp.zeros((n_tiles * tm,), F32).at[pos].set(wv).reshape(-1, 1)
    y = _experts(h.reshape(tokens * ROW_TILE, ROW_LANES), tile_expert, src_rows, row_w, w_gu, w_down, tm)
    return _combine(x, y, pos.reshape(tokens, 2), gate, min(256, s))


def _split_mod(mod):
    return [mod[..., j * D_MODEL:(j + 1) * D_MODEL] for j in range(6)]


def kernel(x_prompt, x_sample, c_prompt, c_sample, cache_a_k, cache_a_v, cache_b_k, cache_b_v, cache_c_k, cache_c_v, rel_bias_table, norm_mix, norm_ffn, w_ada, b_ada, w_in_a, q_gain_a, k_gain_a, sinks_a, w_out_a, w_in_b, q_gain_b, k_gain_b, lam_q1, lam_k1, lam_q2, lam_k2, sub_gain_b, w_out_b, w_in_c, w_out_c, w_group, b_group, w_router, b_router, w_gate, w_up, w_down):
    nb, seq, d = x_prompt.shape
    db, dt, _ = x_sample.shape
    past_len = cache_b_k.shape[2]
    a_cache = cache_a_k.shape[2]
    ns = db * dt

    n_c = nb + db
    c_rows = -(-n_c // 16) * 16
    c_all = jnp.pad(jnp.concatenate([c_prompt, c_sample], axis=0).astype(F32), ((0, c_rows - n_c), (0, 0)))
    mods = _ada_mod(c_all, w_ada, b_ada)

    xp = x_prompt
    xs = x_sample.reshape(1, ns, d)
    st_p, st_s = [], []
    for l in range(DEPTH):
        i, kind = l // N_MIXERS, l % N_MIXERS
        mp = _split_mod(mods[l, :nb][:, None, :])
        ms = _split_mod(jnp.repeat(mods[l, nb:n_c], dt, axis=0)[None])
        hp = _norm_mod(xp, norm_mix[l], mp[1], mp[0])
        hs = _norm_mod(xs, norm_mix[l], ms[1], ms[0])

        if kind == 0:
            nq, nk = A_HEADS * HEAD_DIM, A_KV_HEADS * HEAD_DIM
            wq = w_in_a[i][:, :nq].astype(BF16)
            wk = w_in_a[i][:, nq:nq + nk].astype(BF16)
            wv = w_in_a[i][:, nq + nk:].astype(BF16)
            wo = w_out_a[i].astype(BF16)
            q = _proj(hp, wq, q_gain_a[i], QK_SCALE)
            k32, kbf = _proj(hp, wk, k_gain_a[i], out32=True)
            v32, vbf = _proj(hp, wv, out32=True)
            o = _attn_a_prompt(q, kbf, vbf, rel_bias_table, sinks_a[i])
            xp = _out_res(o, wo, xp, mp[2])
            st_p.append((k32[:, -a_cache:].reshape(nb, a_cache, A_KV_HEADS, HEAD_DIM),
                         v32[:, -a_cache:].reshape(nb, a_cache, A_KV_HEADS, HEAD_DIM)))
            q = _proj(hs, wq, q_gain_a[i], QK_SCALE).reshape(db, dt, nq)
            kn = _proj(hs, wk, k_gain_a[i], out32=True, outbf=False).reshape(db, dt, nk)
            vn = _proj(hs, wv, out32=True, outbf=False).reshape(db, dt, nk)
            ck = cache_a_k[i].reshape(db, a_cache, nk)
            cv = cache_a_v[i].reshape(db, a_cache, nk)
            o = _attn_a_sample(q, ck, cv, kn, vn, rel_bias_table, sinks_a[i], past_len)
            xs = _out_res(o.reshape(1, ns, d), wo, xs, ms[2])
            kk = jnp.concatenate([ck, kn], axis=1)[:, -a_cache:]
            vv = jnp.concatenate([cv, vn], axis=1)[:, -a_cache:]
            st_s.append((kk.reshape(db, a_cache, A_KV_HEADS, HEAD_DIM),
                         vv.reshape(db, a_cache, A_KV_HEADS, HEAD_DIM)))
        elif kind == 1:
            nq = B_HEADS * 2 * HEAD_DIM
            lam_init = _lambda_init(l)
            wq = w_in_b[i][:, :nq].astype(BF16)
            wk = w_in_b[i][:, nq:2 * nq].astype(BF16)
            wv = w_in_b[i][:, 2 * nq:].astype(BF16)
            wo = w_out_b[i].astype(BF16)
            lam_rows = jnp.stack([lam_q1[i], lam_k1[i], lam_q2[i], lam_k2[i]]).astype(F32)
            q = _proj(hp, wq, q_gain_b[i], QK_SCALE)
            k32, kbf = _proj(hp, wk, k_gain_b[i], out32=True)
            v32, vbf = _proj(hp, wv, out32=True)
            o = _attn_b_prompt(q, kbf, vbf, rel_bias_table, lam_rows, sub_gain_b[i], lam_init)
            xp = _out_res(o, wo, xp, mp[2])
            st_p.append((k32.reshape(nb, seq, B_HEADS, 2, HEAD_DIM), v32.reshape(nb, seq, B_HEADS, B_VDIM)))
            q = _proj(hs, wq, q_gain_b[i], QK_SCALE).reshape(db, dt, nq)
            kn = _proj(hs, wk, k_gain_b[i], out32=True, outbf=False).reshape(db, dt, nq)
            vn = _proj(hs, wv, out32=True, outbf=False).reshape(db, dt, nq)
            o = _attn_b_sample(q, cache_b_k[i].reshape(db, past_len, nq), cache_b_v[i].reshape(db, past_len, nq),
                               kn, vn, rel_bias_table, lam_rows, sub_gain_b[i], lam_init, past_len)
            xs = _out_res(o.reshape(1, ns, d), wo, xs, ms[2])
            st_s.append((kn.reshape(db, dt, B_HEADS, 2, HEAD_DIM), vn.reshape(db, dt, B_HEADS, B_VDIM)))
        else:
            wq = w_in_c[i][:, :d].astype(BF16)
            wk = w_in_c[i][:, d:2 * d].astype(BF16)
            wv = w_in_c[i][:, 2 * d:].astype(BF16)
            wo = w_out_c[i].astype(BF16)
            q = _proj(hp, wq, None, QK_SCALE)
            k32, kbf = _proj(hp, wk, out32=True)
            v32, vbf = _proj(hp, wv, out32=True)
            o = _attn_c_prompt(q, kbf, vbf)
            xp = _out_res(o, wo, xp, mp[2])
            st_p.append((k32.reshape(nb, seq, C_HEADS, HEAD_DIM), v32.reshape(nb, seq, C_HEADS, HEAD_DIM)))
            q = _proj(hs, wq, None, QK_SCALE).reshape(db, dt, d)
            kn = _proj(hs, wk, out32=True, outbf=False).reshape(db, dt, d)
            vn = _proj(hs, wv, out32=True, outbf=False).reshape(db, dt, d)
            o = _attn_c_sample(q, cache_c_k[i].reshape(db, past_len, d), cache_c_v[i].reshape(db, past_len, d),
                               kn, vn)
            xs = _out_res(o.reshape(1, ns, d), wo, xs, ms[2])
            st_s.append((kn.reshape(db, dt, C_HEADS, HEAD_DIM), vn.reshape(db, dt, C_HEADS, HEAD_DIM)))

        w_cat, b_row = _router_weights(w_group[l], b_group[l], w_router[l], b_router[l])
        moe_w = (w_cat, b_row,
                 jnp.concatenate([w_gate[l], w_up[l]], axis=-1).astype(BF16),
                 w_down[l].astype(BF16))
        xp = _moe(xp, norm_ffn[l], mp[4], mp[3], mp[5], moe_w)
        xs = _moe(xs, norm_ffn[l], ms[4], ms[3], ms[5], moe_w)

    def collect(states, kind, j):
        return jnp.stack([states[l][j] for l in range(DEPTH) if l % N_MIXERS == kind])

    return (xp, xs.reshape(db, dt, d),
            collect(st_p, 0, 0), collect(st_p, 0, 1),
            collect(st_p, 1, 0), collect(st_p, 1, 1),
            collect(st_p, 2, 0), collect(st_p, 2, 1),
            collect(st_s, 0, 0), collect(st_s, 0, 1),
            collect(st_s, 1, 0), collect(st_s, 1, 1),
            collect(st_s, 2, 0), collect(st_s, 2, 1))
```

```python
import functools
import math

import numpy as np
import jax
import jax.numpy as jnp
from jax import lax
from jax.experimental import pallas as pl
from jax.experimental.pallas import tpu as pltpu

F32 = jnp.float32
BF16 = jnp.bfloat16

D_MODEL = 2048
DEPTH = 4
CHUNK = 64
HEAD_DIM = 128
N_MIXERS = 3
A_HEADS = 16
A_KV_HEADS = 4
A_GROUP = A_HEADS // A_KV_HEADS
WINDOW_CHUNKS = 2
BAND = (WINDOW_CHUNKS + 1) * CHUNK
B_HEADS = 8
B_VDIM = 2 * HEAD_DIM
C_HEADS = 16
N_BUCKETS = 32
MAX_DISTANCE = 128
N_GROUPS = 4
EXPERTS_PER_GROUP = 4
N_EXPERTS = N_GROUPS * EXPERTS_PER_GROUP
D_EXPERT = D_MODEL // 4
EPS = 1e-6
QK_SCALE = HEAD_DIM ** -0.5
LOG2E = math.log2(math.e)
ROW_CHUNK = 32

LANES = 128
MASKED = -1e30
VMEM_LIMIT = 56 * 1024 * 1024

NT_DIMS = (((1,), (1,)), ((), ()))


def _cparams(*sem):
    return pltpu.CompilerParams(dimension_semantics=sem, vmem_limit_bytes=VMEM_LIMIT)


def _lambda_init(layer):
    return 0.8 - 0.6 * math.exp(-0.3 * layer)


def _t5_bucket_np(rel):
    half = N_BUCKETS // 2
    max_exact = half // 2
    n = np.abs(rel)
    nf = np.maximum(n, 1).astype(np.float32)
    large = max_exact + (np.log(nf / np.float32(max_exact)) / np.float32(math.log(MAX_DISTANCE / max_exact))
                         * np.float32(half - max_exact)).astype(np.int32)
    large = np.minimum(large, half - 1)
    return np.where(rel > 0, half, 0) + np.where(n < max_exact, n, large)


def _bias_tile(table, q_pos, k_pos, valid):
    b = _t5_bucket_np(k_pos[None, :] - q_pos[:, None])
    t = jnp.moveaxis(table.astype(F32)[jnp.asarray(b)], -1, 0)
    return jnp.where(jnp.asarray(valid)[None], t, MASKED)


def _mod_spec(mod, tm, tn, imap_rows, imap_one):
    if mod.shape[1] == 1:
        return pl.BlockSpec((None, 1, tn), imap_one)
    return pl.BlockSpec((None, tm, tn), imap_rows)


def _ada_kernel(c_ref, w_ref, b_ref, o_ref):
    c = c_ref[...]
    a = (c / (1.0 + jnp.exp(-c))).astype(BF16)
    o_ref[...] = jnp.dot(a, w_ref[...].astype(BF16), preferred_element_type=F32) + b_ref[...]


def _ada_mod(c_all, w_ada, b_ada):
    rows = c_all.shape[0]
    depth, d, n = w_ada.shape
    tn = 1024
    return pl.pallas_call(
        _ada_kernel,
        grid=(depth, n // tn),
        in_specs=[pl.BlockSpec((rows, d), lambda l, j: (0, 0)),
                  pl.BlockSpec((None, d, tn), lambda l, j: (l, 0, j)),
                  pl.BlockSpec((None, 1, tn), lambda l, j: (l, 0, j))],
        out_specs=pl.BlockSpec((None, rows, tn), lambda l, j: (l, 0, j)),
        out_shape=jax.ShapeDtypeStruct((depth, rows, n), F32),
        compiler_params=_cparams("parallel", "parallel"),
        name="ada_mod",
    )(c_all, w_ada, b_ada.reshape(depth, 1, n))


def _norm_mod_f32(x, g, sc, sh):
    y = x * lax.rsqrt(jnp.mean(x * x, axis=-1, keepdims=True) + EPS) * g
    return y * (1.0 + sc) + sh


def _norm_mod_kernel(x_ref, g_ref, sc_ref, sh_ref, o_ref):
    o_ref[...] = _norm_mod_f32(x_ref[...], g_ref[...], sc_ref[...], sh_ref[...]).astype(o_ref.dtype)


def _norm_mod(x, gain, sc, sh):
    b, s, d = x.shape
    tm = min(512, s)
    rows = lambda bb, i: (bb, i, 0)
    one = lambda bb, i: (bb, 0, 0)
    return pl.pallas_call(
        _norm_mod_kernel,
        grid=(b, s // tm),
        in_specs=[pl.BlockSpec((None, tm, d), rows),
                  pl.BlockSpec((1, d), lambda bb, i: (0, 0)),
                  _mod_spec(sc, tm, d, rows, one),
                  _mod_spec(sh, tm, d, rows, one)],
        out_specs=pl.BlockSpec((None, tm, d), rows),
        out_shape=jax.ShapeDtypeStruct((b, s, d), BF16),
        compiler_params=_cparams("parallel", "parallel"),
        name="norm_mod",
    )(x, gain.reshape(1, d), sc, sh)


def _proj_kernel(*refs, has_gain, scale, out32, outbf):
    x_ref, w_ref = refs[0], refs[1]
    pos = 2
    g_ref = None
    if has_gain:
        g_ref = refs[pos]
        pos += 1
    o32_ref = obf_ref = None
    if out32:
        o32_ref = refs[pos]
        pos += 1
    if outbf:
        obf_ref = refs[pos]
    acc = jnp.dot(x_ref[...], w_ref[...], preferred_element_type=F32)
    tn = acc.shape[1]
    for c in range(tn // HEAD_DIM):
        sl = slice(c * HEAD_DIM, (c + 1) * HEAD_DIM)
        t = acc[:, sl]
        if has_gain:
            t = t * lax.rsqrt(jnp.mean(t * t, axis=-1, keepdims=True) + EPS) * g_ref[...]
        if out32:
            o32_ref[:, sl] = t
        if outbf:
            obf_ref[:, sl] = (t * scale).astype(BF16) if scale != 1.0 else t.astype(BF16)


def _proj(x, w, gain=None, scale=1.0, out32=False, outbf=True):
    b, s, k = x.shape
    n = w.shape[1]
    tm = min(1024, s)
    tn = min(1024, n)
    in_specs = [pl.BlockSpec((None, tm, k), lambda j, bb, i: (bb, i, 0)),
                pl.BlockSpec((k, tn), lambda j, bb, i: (0, j))]
    args = [x, w]
    if gain is not None:
        in_specs.append(pl.BlockSpec((1, HEAD_DIM), lambda j, bb, i: (0, 0)))
        args.append(gain.reshape(1, HEAD_DIM).astype(F32))
    out_specs, out_shape = [], []
    for flag, dt in ((out32, F32), (outbf, BF16)):
        if flag:
            out_specs.append(pl.BlockSpec((None, tm, tn), lambda j, bb, i: (bb, i, j)))
            out_shape.append(jax.ShapeDtypeStruct((b, s, n), dt))
    outs = pl.pallas_call(
        functools.partial(_proj_kernel, has_gain=gain is not None, scale=scale, out32=out32, outbf=outbf),
        grid=(n // tn, b, s // tm),
        in_specs=in_specs, out_specs=out_specs, out_shape=out_shape,
        compiler_params=_cparams("parallel", "parallel", "parallel"),
        name="proj",
    )(*args)
    return outs if len(outs) > 1 else outs[0]


def _out_res_kernel(o_ref, w_ref, x_ref, g_ref, y_ref):
    acc = jnp.dot(o_ref[...], w_ref[...], preferred_element_type=F32)
    y_ref[...] = x_ref[...] + g_ref[...] * acc


def _out_res(o, w, x, gate):
    b, s, k = o.shape
    n = w.shape[1]
    tm = min(1024, s)
    tn = min(1024, n)
    rows = lambda j, bb, i: (bb, i, j)
    one = lambda j, bb, i: (bb, 0, j)
    return pl.pallas_call(
        _out_res_kernel,
        grid=(n // tn, b, s // tm),
        in_specs=[pl.BlockSpec((None, tm, k), lambda j, bb, i: (bb, i, 0)),
                  pl.BlockSpec((k, tn), lambda j, bb, i: (0, j)),
                  pl.BlockSpec((None, tm, tn), rows),
                  _mod_spec(gate, tm, tn, rows, one)],
        out_specs=pl.BlockSpec((None, tm, tn), rows),
        out_shape=jax.ShapeDtypeStruct((b, s, n), F32),
        compiler_params=_cparams("parallel", "parallel", "parallel"),
        name="out_res",
    )(o, w, x, gate)


def _sink_attend(q, k_parts, v_parts, bias_parts, sink):
    s_parts = [lax.dot_general(q, kp, NT_DIMS, preferred_element_type=F32) + bp
               for kp, bp in zip(k_parts, bias_parts)]
    m = sink
    for sp in s_parts:
        m = jnp.maximum(m, jnp.max(sp, axis=-1, keepdims=True))
    denom = jnp.exp(sink - m)
    e_parts = []
    for sp in s_parts:
        e = jnp.exp(sp - m)
        denom = denom + jnp.sum(e, axis=-1, keepdims=True)
        e_parts.append(e)
    inv = 1.0 / denom
    o = None
    for e, vp in zip(e_parts, v_parts):
        t = jnp.dot((e * inv).astype(BF16), vp, preferred_element_type=F32)
        o = t if o is None else o + t
    return o


def _attn_a_prompt_kernel(q_ref, kp_ref, k_ref, vp_ref, v_ref, bias_ref, bias0_ref, bias1_ref, sink_ref, o_ref,
                          *, tq):
    i = pl.program_id(2)
    kcat = jnp.concatenate([kp_ref[...], k_ref[...]], axis=0)
    vcat = jnp.concatenate([vp_ref[...], v_ref[...]], axis=0)
    sink = sink_ref[...]
    first = i == 0
    for c in range(tq // CHUNK):
        rs = slice(c * CHUNK, (c + 1) * CHUNK)
        qc = jnp.concatenate([q_ref[rs, g * HEAD_DIM:(g + 1) * HEAD_DIM] for g in range(A_GROUP)], axis=0)
        kb = kcat[c * CHUNK:c * CHUNK + BAND]
        vb = vcat[c * CHUNK:c * CHUNK + BAND]
        bias = bias_ref[...]
        if c == 0:
            bias = jnp.where(first, bias0_ref[...], bias)
        elif c == 1:
            bias = jnp.where(first, bias1_ref[...], bias)
        o = _sink_attend(qc, [kb], [vb], [bias], sink)
        for g in range(A_GROUP):
            o_ref[rs, g * HEAD_DIM:(g + 1) * HEAD_DIM] = o[g * CHUNK:(g + 1) * CHUNK].astype(o_ref.dtype)


def _attn_a_prompt(q, k, v, table, sinks):
    b, s, _ = q.shape
    tq = min(512, s)
    prev = WINDOW_CHUNKS * CHUNK
    r = tq // prev
    k_loc = np.arange(BAND)
    q_loc = prev + np.arange(CHUNK)

    def tiles(first_key):
        valid = np.broadcast_to((k_loc >= first_key)[None, :], (CHUNK, BAND))
        t = _bias_tile(table, q_loc, k_loc, valid)
        return t.reshape(A_KV_HEADS, A_GROUP * CHUNK, BAND)

    bias, bias0, bias1 = tiles(0), tiles(prev), tiles(CHUNK)
    sink_col = jnp.repeat(sinks.astype(F32).reshape(A_KV_HEADS, A_GROUP), CHUNK, axis=1)[..., None]
    qmap = lambda bb, h, i: (bb, i, h)
    pmap = lambda bb, h, i: (bb, jnp.maximum(i * r - 1, 0), h)
    bmap = lambda bb, h, i: (h, 0, 0)
    bspec = pl.BlockSpec((None, A_GROUP * CHUNK, BAND), bmap)
    return pl.pallas_call(
        functools.partial(_attn_a_prompt_kernel, tq=tq),
        grid=(b, A_KV_HEADS, s // tq),
        in_specs=[pl.BlockSpec((None, tq, A_GROUP * HEAD_DIM), qmap),
                  pl.BlockSpec((None, prev, HEAD_DIM), pmap),
                  pl.BlockSpec((None, tq, HEAD_DIM), qmap),
                  pl.BlockSpec((None, prev, HEAD_DIM), pmap),
                  pl.BlockSpec((None, tq, HEAD_DIM), qmap),
                  bspec, bspec, bspec,
                  pl.BlockSpec((None, A_GROUP * CHUNK, 1), bmap)],
        out_specs=pl.BlockSpec((None, tq, A_GROUP * HEAD_DIM), qmap),
        out_shape=jax.ShapeDtypeStruct((b, s, D_MODEL), BF16),
        compiler_params=_cparams("parallel", "parallel", "parallel"),
        name="attn_a_prompt",
    )(q, k, k, v, v, bias, bias0, bias1, sink_col)


def _attn_a_sample_kernel(q_ref, kc_ref, kn_ref, vc_ref, vn_ref, bc_ref, bn_ref, sink_ref, o_ref, *, t):
    qs = jnp.concatenate([q_ref[:, g * HEAD_DIM:(g + 1) * HEAD_DIM] for g in range(A_GROUP)], axis=0)
    o = _sink_attend(qs,
                     [kc_ref[...].astype(BF16), kn_ref[...].astype(BF16)],
                     [vc_ref[...].astype(BF16), vn_ref[...].astype(BF16)],
                     [bc_ref[...], bn_ref[...]], sink_ref[...])
    for g in range(A_GROUP):
        o_ref[:, g * HEAD_DIM:(g + 1) * HEAD_DIM] = o[g * t:(g + 1) * t].astype(o_ref.dtype)


def _attn_a_sample(q, ck, cv, kn, vn, table, sinks, past_len):
    b, t, _ = q.shape
    cache = ck.shape[1]
    q_pos = past_len + np.arange(t)
    k_pos = np.concatenate([past_len - cache + np.arange(cache), q_pos])
    qc, kc = q_pos[:, None] // CHUNK, k_pos[None, :] // CHUNK
    valid = (kc <= qc) & (kc >= qc - WINDOW_CHUNKS)
    bias = _bias_tile(table, q_pos, k_pos, valid).reshape(A_KV_HEADS, A_GROUP * t, cache + t)
    sink_col = jnp.repeat(sinks.astype(F32).reshape(A_KV_HEADS, A_GROUP), t, axis=1)[..., None]
    hmap = lambda bb, h: (bb, 0, h)
    bmap = lambda bb, h: (h, 0, 0)
    return pl.pallas_call(
        functools.partial(_attn_a_sample_kernel, t=t),
        grid=(b, A_KV_HEADS),
        in_specs=[pl.BlockSpec((None, t, A_GROUP * HEAD_DIM), hmap),
                  pl.BlockSpec((None, cache, HEAD_DIM), hmap),
                  pl.BlockSpec((None, t, HEAD_DIM), hmap),
                  pl.BlockSpec((None, cache, HEAD_DIM), hmap),
                  pl.BlockSpec((None, t, HEAD_DIM), hmap),
                  pl.BlockSpec((None, A_GROUP * t, cache), bmap),
                  pl.BlockSpec((None, A_GROUP * t, t), bmap),
                  pl.BlockSpec((None, A_GROUP * t, 1), bmap)],
        out_specs=pl.BlockSpec((None, t, A_GROUP * HEAD_DIM), hmap),
        out_shape=jax.ShapeDtypeStruct((b, t, D_MODEL), BF16),
        compiler_params=_cparams("parallel", "parallel"),
        name="attn_a_sample",
    )(q, ck, kn, cv, vn, bias[..., :cache], bias[..., cache:], sink_col)


def _diff_lambda(lam_ref, lam_init):
    lp = lam_ref[...]
    a = jnp.sum(lp[0:1] * lp[1:2], axis=-1, keepdims=True)
    c = jnp.sum(lp[2:3] * lp[3:4], axis=-1, keepdims=True)
    return jnp.exp(a) - jnp.exp(c) + lam_init


def _finish_b(o0, o1, lam, gsub, lam_init):
    o = o0 - lam * o1
    o = o * lax.rsqrt(jnp.mean(o * o, axis=-1, keepdims=True) + EPS) * gsub
    return o * (1.0 - lam_init)


B_HEADS_PER_STEP = 1


def _attn_b_prompt_kernel(q_ref, k_ref, v_ref, bias_ref, far_ref, lam_ref, gsub_ref, o_ref,
                          m_sc, l_sc, acc_sc, *, t, lam_init):
    i = pl.program_id(2)
    m_sc[...] = jnp.full(m_sc.shape, MASKED, F32)
    l_sc[...] = jnp.zeros(l_sc.shape, F32)
    acc_sc[...] = jnp.zeros(acc_sc.shape, F32)

    n_maps = 2 * B_HEADS_PER_STEP

    def step(j, kind):
        rows = pl.ds(pl.multiple_of(j * t, t), t)
        for mi in range(n_maps):
            hd = mi // 2
            sl = slice(mi * HEAD_DIM, (mi + 1) * HEAD_DIM)
            s = lax.dot_general(q_ref[:, sl], k_ref[rows, sl], NT_DIMS, preferred_element_type=F32)
            if kind is not None:
                s = s + bias_ref[mi, kind]
            m_old = m_sc[mi]
            m_new = jnp.maximum(m_old, jnp.max(s, axis=-1, keepdims=True))
            alpha = jnp.exp2(m_old - m_new)
            p = jnp.exp2(s - m_new)
            l_sc[mi] = alpha * l_sc[mi] + jnp.sum(p, axis=-1, keepdims=True)
            vb = v_ref[rows, hd * B_VDIM:(hd + 1) * B_VDIM]
            acc_sc[mi] = alpha * acc_sc[mi] + jnp.dot(p.astype(BF16), vb, preferred_element_type=F32)
            m_sc[mi] = m_new

    def far_body(j, carry):
        step(j, None)
        return carry

    lax.fori_loop(0, jnp.maximum(i - 1, 0), far_body, 0)
    for mi in range(n_maps):
        m_sc[mi] = m_sc[mi] + far_ref[mi][:, 0:1]

    @pl.when(i >= 1)
    def _():
        step(i - 1, 1)

    step(i, 0)
    lam = _diff_lambda(lam_ref, lam_init)
    for hd in range(B_HEADS_PER_STEP):
        o = _finish_b(acc_sc[2 * hd] / l_sc[2 * hd], acc_sc[2 * hd + 1] / l_sc[2 * hd + 1], lam, gsub_ref[...],
                      lam_init)
        o_ref[:, hd * B_VDIM:(hd + 1) * B_VDIM] = o.astype(o_ref.dtype)


def _toeplitz(f, t):
    h = f.shape[0]
    g = jnp.pad(f, ((0, 0), (0, 1)))
    flat = jnp.tile(g, (1, t))[:, :t * (2 * t - 1)]
    return flat.reshape(h, t, 2 * t - 1)[:, :, t - 1:]


def _attn_b_prompt(q, k, v, table, lam_rows, gsub, lam_init):
    b, s, _ = q.shape
    t = min(512, s)
    hp = B_HEADS_PER_STEP
    loc = np.arange(t)
    chunk_ok = jnp.asarray((loc[None, :] // CHUNK) <= (loc[:, None] // CHUNK))
    tab = table.astype(F32) * LOG2E
    rel = np.arange(-(t - 1), t)
    diag = _toeplitz(tab[jnp.asarray(_t5_bucket_np(rel))].T, t)
    diag = jnp.where(chunk_ok[None], diag, MASKED)
    sub = _toeplitz(tab[jnp.asarray(_t5_bucket_np(rel - t))].T, t)
    bias = jnp.stack([diag, sub], axis=1)
    far_rel = -(t + 1 + np.arange(max(s - t, 1)))
    far_bucket = _t5_bucket_np(far_rel)
    assert np.all(far_bucket == far_bucket[0])
    far = jnp.broadcast_to(tab[int(far_bucket[0])].reshape(2 * B_HEADS, 1, 1), (2 * B_HEADS, 1, LANES))
    qmap = lambda bb, h, i: (bb, i, h)
    kmap = lambda bb, h, i: (bb, 0, h)
    return pl.pallas_call(
        functools.partial(_attn_b_prompt_kernel, t=t, lam_init=lam_init),
        grid=(b, B_HEADS // hp, s // t),
        in_specs=[pl.BlockSpec((None, t, hp * B_VDIM), qmap),
                  pl.BlockSpec((None, s, hp * B_VDIM), kmap),
                  pl.BlockSpec((None, s, hp * B_VDIM), kmap),
                  pl.BlockSpec((2 * hp, 2, t, t), lambda bb, h, i: (h, 0, 0, 0)),
                  pl.BlockSpec((2 * hp, 1, LANES), lambda bb, h, i: (h, 0, 0)),
                  pl.BlockSpec((4, HEAD_DIM), lambda bb, h, i: (0, 0)),
                  pl.BlockSpec((1, B_VDIM), lambda bb, h, i: (0, 0))],
        out_specs=pl.BlockSpec((None, t, hp * B_VDIM), qmap),
        out_shape=jax.ShapeDtypeStruct((b, s, D_MODEL), BF16),
        scratch_shapes=[pltpu.VMEM((2 * hp, t, 1), F32), pltpu.VMEM((2 * hp, t, 1), F32),
                        pltpu.VMEM((2 * hp, t, B_VDIM), F32)],
        compiler_params=_cparams("parallel", "parallel", "parallel"),
        name="attn_b_prompt",
    )(q, k, v, bias, far, lam_rows, gsub.reshape(1, B_VDIM).astype(F32))


def _attn_b_sample_kernel(q_ref, kc_ref, kn_ref, vc_ref, vn_ref, bc_ref, bn_ref, lam_ref, gsub_ref, o_ref,
                          *, lam_init):
    kc = kc_ref[...].astype(BF16)
    kn = kn_ref[...].astype(BF16)
    vc = vc_ref[...].astype(BF16)
    vn = vn_ref[...].astype(BF16)
    outs = []
    for mp in range(2):
        sl = slice(mp * HEAD_DIM, (mp + 1) * HEAD_DIM)
        qm = q_ref[:, sl]
        sc = lax.dot_general(qm, kc[:, sl], NT_DIMS, preferred_element_type=F32) + bc_ref[mp]
        sn = lax.dot_general(qm, kn[:, sl], NT_DIMS, preferred_element_type=F32) + bn_ref[mp]
        m = jnp.maximum(jnp.max(sc, axis=-1, keepdims=True), jnp.max(sn, axis=-1, keepdims=True))
        pc = jnp.exp(sc - m)
        pn = jnp.exp(sn - m)
        inv = 1.0 / (jnp.sum(pc, axis=-1, keepdims=True) + jnp.sum(pn, axis=-1, keepdims=True))
        outs.append(jnp.dot((pc * inv).astype(BF16), vc, preferred_element_type=F32)
                    + jnp.dot((pn * inv).astype(BF16), vn, preferred_element_type=F32))
    lam = _diff_lambda(lam_ref, lam_init)
    o_ref[...] = _finish_b(outs[0], outs[1], lam, gsub_ref[...], lam_init).astype(o_ref.dtype)


def _attn_b_sample(q, ck, cv, kn, vn, table, lam_rows, gsub, lam_init, past_len):
    b, t, _ = q.shape
    past = ck.shape[1]
    q_pos = past_len + np.arange(t)
    k_pos = np.arange(past_len + t)
    valid = (k_pos[None, :] // CHUNK) <= (q_pos[:, None] // CHUNK)
    bias = _bias_tile(table, q_pos, k_pos, valid).reshape(B_HEADS, 2, t, past + t)
    hmap = lambda bb, h: (bb, 0, h)
    bmap = lambda bb, h: (h, 0, 0, 0)
    return pl.pallas_call(
        functools.partial(_attn_b_sample_kernel, lam_init=lam_init),
        grid=(b, B_HEADS),
        in_specs=[pl.BlockSpec((None, t, B_VDIM), hmap),
                  pl.BlockSpec((None, past, B_VDIM), hmap),
                  pl.BlockSpec((None, t, B_VDIM), hmap),
                  pl.BlockSpec((None, past, B_VDIM), hmap),
                  pl.BlockSpec((None, t, B_VDIM), hmap),
                  pl.BlockSpec((None, 2, t, past), bmap),
                  pl.BlockSpec((None, 2, t, t), bmap),
                  pl.BlockSpec((4, HEAD_DIM), lambda bb, h: (0, 0)),
                  pl.BlockSpec((1, B_VDIM), lambda bb, h: (0, 0))],
        out_specs=pl.BlockSpec((None, t, B_VDIM), hmap),
        out_shape=jax.ShapeDtypeStruct((b, t, D_MODEL), BF16),
        compiler_params=_cparams("parallel", "parallel"),
        name="attn_b_sample",
    )(q, ck, kn, cv, vn, bias[..., :past], bias[..., past:], lam_rows, gsub.reshape(1, B_VDIM).astype(F32))


def _minus_later_ones(n):
    idx = np.arange(n)
    return jnp.asarray(-(idx[:, None] > idx[None, :]).astype(np.float32), dtype=BF16)


def _sb_block(q, kb, vb, u, run, mask):
    z = lax.dot_general(q, kb, NT_DIMS, preferred_element_type=F32)
    sp = jnp.maximum(z, 0.0) + jnp.log(1.0 + jnp.exp(-jnp.abs(z)))
    if mask is not None:
        sp = jnp.where(mask, sp, 0.0)
    after = jnp.dot(sp.astype(BF16), u, preferred_element_type=F32) + run
    a = jnp.exp((z - sp) + after)
    if mask is not None:
        a = jnp.where(mask, a, 0.0)
    o = jnp.dot(a.astype(BF16), vb, preferred_element_type=F32)
    return o, run - jnp.sum(sp, axis=-1, keepdims=True)


C_HEADS_PER_STEP = 4


def _attn_c_prompt_kernel(q_ref, k_ref, v_ref, u_ref, o_ref, run_sc, acc_sc, *, t):
    i = pl.program_id(2)
    u = u_ref[...]
    row = lax.broadcasted_iota(jnp.int32, (t, t), 0)
    col = lax.broadcasted_iota(jnp.int32, (t, t), 1)
    heads = [slice(h * HEAD_DIM, (h + 1) * HEAD_DIM) for h in range(C_HEADS_PER_STEP)]
    start = pl.multiple_of(i * t, t)
    for h, hs in enumerate(heads):
        o, run = _sb_block(q_ref[:, hs], k_ref[pl.ds(start, t), hs], v_ref[pl.ds(start, t), hs], u,
                           jnp.zeros((t, 1), F32), col < row)
        acc_sc[h] = o
        run_sc[h] = run

    def body(n, carry):
        st = pl.multiple_of((i - 1 - n) * t, t)
        for h, hs in enumerate(heads):
            o, run = _sb_block(q_ref[:, hs], k_ref[pl.ds(st, t), hs], v_ref[pl.ds(st, t), hs], u, run_sc[h], None)
            acc_sc[h] += o
            run_sc[h] = run
        return carry

    lax.fori_loop(0, i, body, 0)
    for h, hs in enumerate(heads):
        o_ref[:, hs] = acc_sc[h].astype(o_ref.dtype)


def _attn_c_prompt(q, k, v):
    b, s, _ = q.shape
    t = min(256, s)
    hp = C_HEADS_PER_STEP
    qmap = lambda bb, h, i: (bb, i, h)
    kmap = lambda bb, h, i: (bb, 0, h)
    return pl.pallas_call(
        functools.partial(_attn_c_prompt_kernel, t=t),
        grid=(b, C_HEADS // hp, s // t),
        in_specs=[pl.BlockSpec((None, t, hp * HEAD_DIM), qmap),
                  pl.BlockSpec((None, s, hp * HEAD_DIM), kmap),
                  pl.BlockSpec((None, s, hp * HEAD_DIM), kmap),
                  pl.BlockSpec((t, t), lambda bb, h, i: (0, 0))],
        out_specs=pl.BlockSpec((None, t, hp * HEAD_DIM), qmap),
        out_shape=jax.ShapeDtypeStruct((b, s, D_MODEL), BF16),
        scratch_shapes=[pltpu.VMEM((hp, t, 1), F32), pltpu.VMEM((hp, t, HEAD_DIM), F32)],
        compiler_params=_cparams("parallel", "parallel", "parallel"),
        name="attn_c_prompt",
    )(q, k, v, _minus_later_ones(t))


def _attn_c_sample_kernel(q_ref, kc_ref, kn_ref, vc_ref, vn_ref, u_ref, o_ref, *, t, past, tk):
    q = q_ref[...]
    u = u_ref[...]
    row = lax.broadcasted_iota(jnp.int32, (t, t), 0)
    col = lax.broadcasted_iota(jnp.int32, (t, t), 1)
    acc, run = _sb_block(q, kn_ref[...].astype(BF16), vn_ref[...].astype(BF16), u[:t, :t],
                         jnp.zeros((t, 1), F32), col < row)
    for j in reversed(range(past // tk)):
        rs = slice(j * tk, (j + 1) * tk)
        o, run = _sb_block(q, kc_ref[rs, :].astype(BF16), vc_ref[rs, :].astype(BF16), u, run, None)
        acc = acc + o
    o_ref[...] = acc.astype(o_ref.dtype)


def _attn_c_sample(q, ck, cv, kn, vn):
    b, t, _ = q.shape
    past = ck.shape[1]
    tk = min(256, past)
    hmap = lambda bb, h: (bb, 0, h)
    return pl.pallas_call(
        functools.partial(_attn_c_sample_kernel, t=t, past=past, tk=tk),
        grid=(b, C_HEADS),
        in_specs=[pl.BlockSpec((None, t, HEAD_DIM), hmap),
                  pl.BlockSpec((None, past, HEAD_DIM), hmap),
                  pl.BlockSpec((None, t, HEAD_DIM), hmap),
                  pl.BlockSpec((None, past, HEAD_DIM), hmap),
                  pl.BlockSpec((None, t, HEAD_DIM), hmap),
                  pl.BlockSpec((tk, tk), lambda bb, h: (0, 0))],
        out_specs=pl.BlockSpec((None, t, HEAD_DIM), hmap),
        out_shape=jax.ShapeDtypeStruct((b, t, D_MODEL), BF16),
        compiler_params=_cparams("parallel", "parallel"),
        name="attn_c_sample",
    )(q, ck, kn, cv, vn, _minus_later_ones(tk))


ROUTE_LANES = LANES
EXPERT_LANE0 = N_GROUPS
ROW_LANES = LANES
ROW_TILE = D_MODEL // ROW_LANES


def _store_row_tiled(ref, val):
    rows = val.shape[0]
    for s in range(ROW_TILE):
        ref[pl.ds(s, rows, stride=ROW_TILE), :] = val[:, s * ROW_LANES:(s + 1) * ROW_LANES]


def _load_row_tiled(ref, rows, s, base=0):
    return ref[pl.ds(base * ROW_TILE + s, rows, stride=ROW_TILE), :]


def _router_kernel(x_ref, g_ref, sc_ref, sh_ref, w_ref, b_ref, h_ref, r_ref):
    h = _norm_mod_f32(x_ref[...], g_ref[...], sc_ref[...], sh_ref[...])
    _store_row_tiled(h_ref, h)
    h_hi = h.astype(BF16)
    h_lo = (h - h_hi.astype(F32)).astype(BF16)
    w = w_ref[...]
    both = jnp.dot(h_hi, w, preferred_element_type=F32)
    logits = (both[:, :ROUTE_LANES] + both[:, ROUTE_LANES:]
              + jnp.dot(h_lo, w[:, :ROUTE_LANES], preferred_element_type=F32) + b_ref[...])
    lane = lax.broadcasted_iota(jnp.int32, logits.shape, 1)
    lanef = lane.astype(F32)

    def first_lane_of_max(v, vmax):
        return jnp.min(jnp.where(v == vmax, lanef, float(ROUTE_LANES)), axis=-1, keepdims=True)

    lg = jnp.where(lane < N_GROUPS, logits, MASKED)
    mg = jnp.max(lg, axis=-1, keepdims=True)
    gate = 1.0 / jnp.sum(jnp.exp(lg - mg), axis=-1, keepdims=True)
    gi = first_lane_of_max(lg, mg)
    lo = EXPERT_LANE0 + EXPERTS_PER_GROUP * gi
    le = jnp.where((lanef >= lo) & (lanef < lo + EXPERTS_PER_GROUP), logits, MASKED)
    v1 = jnp.max(le, axis=-1, keepdims=True)
    i1 = first_lane_of_max(le, v1)
    le2 = jnp.where(lanef == i1, MASKED, le)
    v2 = jnp.max(le2, axis=-1, keepdims=True)
    i2 = first_lane_of_max(le2, v2)
    e21 = jnp.exp(v2 - v1)
    w1 = gate / (1.0 + e21)
    w2 = w1 * e21
    r_ref[...] = jnp.where(lane == 0, i1 - EXPERT_LANE0,
                           jnp.where(lane == 1, i2 - EXPERT_LANE0,
                                     jnp.where(lane == 2, w1, jnp.where(lane == 3, w2, 0.0))))


def _router(x, gain, sc, sh, w_cat, b_row):
    b, s, d = x.shape
    tm = min(512, s)
    rows = lambda bb, i: (bb, i, 0)
    one = lambda bb, i: (bb, 0, 0)
    return pl.pallas_call(
        _router_kernel,
        grid=(b, s // tm),
        in_specs=[pl.BlockSpec((None, tm, d), rows),
                  pl.BlockSpec((1, d), lambda bb, i: (0, 0)),
                  _mod_spec(sc, tm, d, rows, one),
                  _mod_spec(sh, tm, d, rows, one),
                  pl.BlockSpec((d, 2 * ROUTE_LANES), lambda bb, i: (0, 0)),
                  pl.BlockSpec((1, ROUTE_LANES), lambda bb, i: (0, 0))],
        out_specs=[pl.BlockSpec((None, tm * ROW_TILE, ROW_LANES), rows),
                   pl.BlockSpec((None, tm, ROUTE_LANES), rows)],
        out_shape=[jax.ShapeDtypeStruct((b, s * ROW_TILE, ROW_LANES), F32),
                   jax.ShapeDtypeStruct((b, s, ROUTE_LANES), F32)],
        compiler_params=_cparams("parallel", "parallel"),
        name="moe_router",
    )(x, gain.reshape(1, d), sc, sh, w_cat, b_row)


def _router_weights(w_group, b_group, w_router, b_router):
    w = jnp.concatenate([w_group, jnp.moveaxis(w_router, 0, 1).reshape(D_MODEL, N_EXPERTS)], axis=1)
    w = jnp.pad(w.astype(F32), ((0, 0), (0, ROUTE_LANES - w.shape[1])))
    hi = w.astype(BF16)
    lo = (w - hi.astype(F32)).astype(BF16)
    bias = jnp.concatenate([b_group, b_router.reshape(N_EXPERTS)]).astype(F32)
    bias = jnp.pad(bias, (0, ROUTE_LANES - bias.shape[0])).reshape(1, ROUTE_LANES)
    return jnp.concatenate([hi, lo], axis=1), bias


def _row_gather(src_hbm, idx_ref, n_rows, dst, sem):
    def body(r, carry):
        src = pl.multiple_of(idx_ref[0, r] * ROW_TILE, ROW_TILE)
        pltpu.make_async_copy(src_hbm.at[pl.ds(src, ROW_TILE)],
                              dst.at[pl.ds(pl.multiple_of(r * ROW_TILE, ROW_TILE), ROW_TILE)], sem).start()
        return carry
    lax.fori_loop(0, n_rows, body, 0, unroll=8)


def _row_gather_wait(src_hbm, n_rows, dst, sem):
    pltpu.make_async_copy(src_hbm.at[pl.ds(0, n_rows * ROW_TILE)], dst, sem).wait()


def _expert_kernel(te_ref, idx_ref, idx_next_ref, h_hbm, wgu_ref, wd_ref, y_ref, buf, sem, *, tm):
    i = pl.program_id(0)
    n = pl.num_programs(0)
    slot = i % 2

    @pl.when(i == 0)
    def _():
        _row_gather(h_hbm, idx_ref, tm, buf.at[0], sem.at[0])

    @pl.when(i + 1 < n)
    def _():
        _row_gather(h_hbm, idx_next_ref, tm, buf.at[1 - slot], sem.at[1 - slot])

    _row_gather_wait(h_hbm, tm, buf.at[slot], sem.at[slot])
    xb = buf.at[slot]
    x = jnp.concatenate([_load_row_tiled(xb, tm, s).astype(BF16) for s in range(ROW_TILE)], axis=1)
    gu = jnp.dot(x, wgu_ref[...], preferred_element_type=F32)
    gate, up = gu[:, :D_EXPERT], gu[:, D_EXPERT:]
    hid = (gate / (1.0 + jnp.exp(-gate)) * up).astype(BF16)
    _store_row_tiled(y_ref, jnp.dot(hid, wd_ref[...], preferred_element_type=F32))


def _experts(h_rt, tile_expert, src_rows, w_gu, w_down, tm):
    n_tiles = tile_expert.shape[0]
    d = D_MODEL
    grid_spec = pltpu.PrefetchScalarGridSpec(
        num_scalar_prefetch=1,
        grid=(n_tiles,),
        in_specs=[pl.BlockSpec((None, 1, tm), lambda i, te: (i, 0, 0), memory_space=pltpu.SMEM),
                  pl.BlockSpec((None, 1, tm), lambda i, te: (jnp.minimum(i + 1, n_tiles - 1), 0, 0),
                               memory_space=pltpu.SMEM),
                  pl.BlockSpec(memory_space=pl.ANY),
                  pl.BlockSpec((None, d, 2 * D_EXPERT), lambda i, te: (te[i], 0, 0)),
                  pl.BlockSpec((None, D_EXPERT, d), lambda i, te: (te[i], 0, 0))],
        out_specs=pl.BlockSpec((tm * ROW_TILE, ROW_LANES), lambda i, te: (i, 0)),
        scratch_shapes=[pltpu.VMEM((2, tm * ROW_TILE, ROW_LANES), F32), pltpu.SemaphoreType.DMA((2,))],
    )
    idx3 = src_rows.reshape(n_tiles, 1, tm)
    return pl.pallas_call(
        functools.partial(_expert_kernel, tm=tm),
        grid_spec=grid_spec,
        out_shape=jax.ShapeDtypeStruct((n_tiles * tm * ROW_TILE, ROW_LANES), F32),
        compiler_params=_cparams("arbitrary"),
        name="moe_experts",
    )(tile_expert, idx3, idx3, h_rt, w_gu, w_down)


def _combine_kernel(pos_ref, pos_next_ref, y_hbm, x_ref, g_ref, r_ref, o_ref, buf, sem, *, tc):
    i = pl.program_id(0) * pl.num_programs(1) + pl.program_id(1)
    n = pl.num_programs(0) * pl.num_programs(1)
    slot = i % 2

    @pl.when(i == 0)
    def _():
        _row_gather(y_hbm, pos_ref, 2 * tc, buf.at[0], sem.at[0])

    @pl.when(i + 1 < n)
    def _():
        _row_gather(y_hbm, pos_next_ref, 2 * tc, buf.at[1 - slot], sem.at[1 - slot])

    _row_gather_wait(y_hbm, 2 * tc, buf.at[slot], sem.at[slot])
    yb = buf.at[slot]
    w1 = r_ref[:, 2:3]
    w2 = r_ref[:, 3:4]
    for s in range(ROW_TILE):
        sl = slice(s * ROW_LANES, (s + 1) * ROW_LANES)
        y = w1 * _load_row_tiled(yb, tc, s) + w2 * _load_row_tiled(yb, tc, s, base=tc)
        o_ref[:, sl] = x_ref[:, sl] + g_ref[:, sl] * y


def _combine(x, y_rt, pos, gate, route, tc):
    b, s, d = x.shape
    nt = s // tc
    n = b * nt
    pos3 = pos.reshape(n, tc, 2).transpose(0, 2, 1).reshape(n, 1, 2 * tc)
    rows = lambda bb, i: (bb, i, 0)
    one = lambda bb, i: (bb, 0, 0)
    return pl.pallas_call(
        functools.partial(_combine_kernel, tc=tc),
        grid=(b, nt),
        in_specs=[pl.BlockSpec((None, 1, 2 * tc), lambda bb, i: (bb * nt + i, 0, 0), memory_space=pltpu.SMEM),
                  pl.BlockSpec((None, 1, 2 * tc), lambda bb, i: (jnp.minimum(bb * nt + i + 1, n - 1), 0, 0),
                               memory_space=pltpu.SMEM),
                  pl.BlockSpec(memory_space=pl.ANY),
                  pl.BlockSpec((None, tc, d), rows),
                  _mod_spec(gate, tc, d, rows, one),
                  pl.BlockSpec((None, tc, ROUTE_LANES), rows)],
        out_specs=pl.BlockSpec((None, tc, d), rows),
        out_shape=jax.ShapeDtypeStruct((b, s, d), F32),
        scratch_shapes=[pltpu.VMEM((2, 2 * tc * ROW_TILE, ROW_LANES), F32), pltpu.SemaphoreType.DMA((2,))],
        compiler_params=_cparams("arbitrary", "arbitrary"),
        name="moe_combine",
    )(pos3, pos3, y_rt, x, gate, route)


def _moe(x, gain, sc, sh, gate, wts):
    w_cat, b_row, w_gu, w_down = wts
    b, s, d = x.shape
    tokens = b * s
    h, route = _router(x, gain, sc, sh, w_cat, b_row)
    ids = route.reshape(tokens, ROUTE_LANES)[:, 0:2].astype(jnp.int32).reshape(-1)
    tm = 256 if tokens >= 4096 else 128
    n_assign = 2 * tokens
    n_tiles = (n_assign + N_EXPERTS * (tm - 1)) // tm + 1
    onehot = (ids[:, None] == jnp.arange(N_EXPERTS)[None, :]).astype(jnp.int32)
    csum = jnp.cumsum(onehot, axis=0)
    counts = csum[-1]
    padded = ((counts + tm - 1) // tm) * tm
    ends = jnp.cumsum(padded)
    pos = jnp.sum(onehot * (csum - 1 + (ends - padded)[None, :]), axis=1)
    tile_expert = jnp.minimum(jnp.searchsorted(ends // tm, jnp.arange(n_tiles), side='right'),
                              N_EXPERTS - 1).astype(jnp.int32)
    src_rows = jnp.zeros((n_tiles * tm,), jnp.int32).at[pos].set(jnp.arange(n_assign, dtype=jnp.int32) // 2)
    y = _experts(h.reshape(tokens * ROW_TILE, ROW_LANES), tile_expert, src_rows, w_gu, w_down, tm)
    return _combine(x, y, pos.reshape(tokens, 2), gate, route, min(256, s))


def _split_mod(mod):
    return [mod[..., j * D_MODEL:(j + 1) * D_MODEL] for j in range(6)]


def kernel(x_prompt, x_sample, c_prompt, c_sample, cache_a_k, cache_a_v, cache_b_k, cache_b_v, cache_c_k, cache_c_v, rel_bias_table, norm_mix, norm_ffn, w_ada, b_ada, w_in_a, q_gain_a, k_gain_a, sinks_a, w_out_a, w_in_b, q_gain_b, k_gain_b, lam_q1, lam_k1, lam_q2, lam_k2, sub_gain_b, w_out_b, w_in_c, w_out_c, w_group, b_group, w_router, b_router, w_gate, w_up, w_down):
    nb, seq, d = x_prompt.shape
    db, dt, _ = x_sample.shape
    past_len = cache_b_k.shape[2]
    a_cache = cache_a_k.shape[2]
    ns = db * dt

    n_c = nb + db
    c_rows = -(-n_c // 16) * 16
    c_all = jnp.pad(jnp.concatenate([c_prompt, c_sample], axis=0).astype(F32), ((0, c_rows - n_c), (0, 0)))
    mods = _ada_mod(c_all, w_ada, b_ada)

    xp = x_prompt
    xs = x_sample.reshape(1, ns, d)
    st_p, st_s = [], []
    for l in range(DEPTH):
        i, kind = l // N_MIXERS, l % N_MIXERS
        mp = _split_mod(mods[l, :nb][:, None, :])
        ms = _split_mod(jnp.repeat(mods[l, nb:n_c], dt, axis=0)[None])
        hp = _norm_mod(xp, norm_mix[l], mp[1], mp[0])
        hs = _norm_mod(xs, norm_mix[l], ms[1], ms[0])

        if kind == 0:
            nq, nk = A_HEADS * HEAD_DIM, A_KV_HEADS * HEAD_DIM
            wq = w_in_a[i][:, :nq].astype(BF16)
            wk = w_in_a[i][:, nq:nq + nk].astype(BF16)
            wv = w_in_a[i][:, nq + nk:].astype(BF16)
            wo = w_out_a[i].astype(BF16)
            q = _proj(hp, wq, q_gain_a[i], QK_SCALE)
            k32, kbf = _proj(hp, wk, k_gain_a[i], out32=True)
            v32, vbf = _proj(hp, wv, out32=True)
            o = _attn_a_prompt(q, kbf, vbf, rel_bias_table, sinks_a[i])
            xp = _out_res(o, wo, xp, mp[2])
            st_p.append((k32[:, -a_cache:].reshape(nb, a_cache, A_KV_HEADS, HEAD_DIM),
                         v32[:, -a_cache:].reshape(nb, a_cache, A_KV_HEADS, HEAD_DIM)))
            q = _proj(hs, wq, q_gain_a[i], QK_SCALE).reshape(db, dt, nq)
            kn = _proj(hs, wk, k_gain_a[i], out32=True, outbf=False).reshape(db, dt, nk)
            vn = _proj(hs, wv, out32=True, outbf=False).reshape(db, dt, nk)
            ck = cache_a_k[i].reshape(db, a_cache, nk)
            cv = cache_a_v[i].reshape(db, a_cache, nk)
            o = _attn_a_sample(q, ck, cv, kn, vn, rel_bias_table, sinks_a[i], past_len)
            xs = _out_res(o.reshape(1, ns, d), wo, xs, ms[2])
            kk = jnp.concatenate([ck, kn], axis=1)[:, -a_cache:]
            vv = jnp.concatenate([cv, vn], axis=1)[:, -a_cache:]
            st_s.append((kk.reshape(db, a_cache, A_KV_HEADS, HEAD_DIM),
                         vv.reshape(db, a_cache, A_KV_HEADS, HEAD_DIM)))
        elif kind == 1:
            nq = B_HEADS * 2 * HEAD_DIM
            lam_init = _lambda_init(l)
            wq = w_in_b[i][:, :nq].astype(BF16)
            wk = w_in_b[i][:, nq:2 * nq].astype(BF16)
            wv = w_in_b[i][:, 2 * nq:].astype(BF16)
            wo = w_out_b[i].astype(BF16)
            lam_rows = jnp.stack([lam_q1[i], lam_k1[i], lam_q2[i], lam_k2[i]]).astype(F32)
            q = _proj(hp, wq, q_gain_b[i], QK_SCALE * LOG2E)
            k32, kbf = _proj(hp, wk, k_gain_b[i], out32=True)
            v32, vbf = _proj(hp, wv, out32=True)
            o = _attn_b_prompt(q, kbf, vbf, rel_bias_table, lam_rows, sub_gain_b[i], lam_init)
            xp = _out_res(o, wo, xp, mp[2])
            st_p.append((k32.reshape(nb, seq, B_HEADS, 2, HEAD_DIM), v32.reshape(nb, seq, B_HEADS, B_VDIM)))
            q = _proj(hs, wq, q_gain_b[i], QK_SCALE).reshape(db, dt, nq)
            kn = _proj(hs, wk, k_gain_b[i], out32=True, outbf=False).reshape(db, dt, nq)
            vn = _proj(hs, wv, out32=True, outbf=False).reshape(db, dt, nq)
            o = _attn_b_sample(q, cache_b_k[i].reshape(db, past_len, nq), cache_b_v[i].reshape(db, past_len, nq),
                               kn, vn, rel_bias_table, lam_rows, sub_gain_b[i], lam_init, past_len)
            xs = _out_res(o.reshape(1, ns, d), wo, xs, ms[2])
            st_s.append((kn.reshape(db, dt, B_HEADS, 2, HEAD_DIM), vn.reshape(db, dt, B_HEADS, B_VDIM)))
        else:
            wq = w_in_c[i][:, :d].astype(BF16)
            wk = w_in_c[i][:, d:2 * d].astype(BF16)
            wv = w_in_c[i][:, 2 * d:].astype(BF16)
            wo = w_out_c[i].astype(BF16)
            q = _proj(hp, wq, None, QK_SCALE)
            k32, kbf = _proj(hp, wk, out32=True)
            v32, vbf = _proj(hp, wv, out32=True)
            o = _attn_c_prompt(q, kbf, vbf)
            xp = _out_res(o, wo, xp, mp[2])
            st_p.append((k32.reshape(nb, seq, C_HEADS, HEAD_DIM), v32.reshape(nb, seq, C_HEADS, HEAD_DIM)))
            q = _proj(hs, wq, None, QK_SCALE).reshape(db, dt, d)
            kn = _proj(hs, wk, out32=True, outbf=False).reshape(db, dt, d)
            vn = _proj(hs, wv, out32=True, outbf=False).reshape(db, dt, d)
            o = _attn_c_sample(q, cache_c_k[i].reshape(db, past_len, d), cache_c_v[i].reshape(db, past_len, d),
                               kn, vn)
            xs = _out_res(o.reshape(1, ns, d), wo, xs, ms[2])
            st_s.append((kn.reshape(db, dt, C_HEADS, HEAD_DIM), vn.reshape(db, dt, C_HEADS, HEAD_DIM)))

        w_cat, b_row = _router_weights(w_group[l], b_group[l], w_router[l], b_router[l])
        moe_w = (w_cat, b_row,
                 jnp.concatenate([w_gate[l], w_up[l]], axis=-1).astype(BF16),
                 w_down[l].astype(BF16))
        xp = _moe(xp, norm_ffn[l], mp[4], mp[3], mp[5], moe_w)
        xs = _moe(xs, norm_ffn[l], ms[4], ms[3], ms[5], moe_w)

    def collect(states, kind, j):
        parts = [states[l][j] for l in range(DEPTH) if l % N_MIXERS == kind]
        return parts[0][None] if len(parts) == 1 else jnp.stack(parts)

    return (xp, xs.reshape(db, dt, d),
            collect(st_p, 0, 0), collect(st_p, 0, 1),
            collect(st_p, 1, 0), collect(st_p, 1, 1),
            collect(st_p, 2, 0), collect(st_p, 2, 1),
            collect(st_s, 0, 0), collect(st_s, 0, 1),
            collect(st_s, 1, 0), collect(st_s, 1, 1),
            collect(st_s, 2, 0), collect(st_s, 2, 1))
```

```python
import functools
import math

import numpy as np
import jax
import jax.numpy as jnp
from jax import lax
from jax.experimental import pallas as pl
from jax.experimental.pallas import tpu as pltpu

F32 = jnp.float32
BF16 = jnp.bfloat16

D_MODEL = 2048
DEPTH = 4
CHUNK = 64
HEAD_DIM = 128
N_MIXERS = 3
A_HEADS = 16
A_KV_HEADS = 4
A_GROUP = A_HEADS // A_KV_HEADS
WINDOW_CHUNKS = 2
BAND = (WINDOW_CHUNKS + 1) * CHUNK
B_HEADS = 8
B_VDIM = 2 * HEAD_DIM
C_HEADS = 16
N_BUCKETS = 32
MAX_DISTANCE = 128
N_GROUPS = 4
EXPERTS_PER_GROUP = 4
N_EXPERTS = N_GROUPS * EXPERTS_PER_GROUP
D_EXPERT = D_MODEL // 4
EPS = 1e-6
QK_SCALE = HEAD_DIM ** -0.5
LOG2E = math.log2(math.e)
ROW_CHUNK = 32

LANES = 128
MASKED = -1e30
VMEM_LIMIT = 56 * 1024 * 1024

NT_DIMS = (((1,), (1,)), ((), ()))


def _cparams(*sem):
    return pltpu.CompilerParams(dimension_semantics=sem, vmem_limit_bytes=VMEM_LIMIT)


def _lambda_init(layer):
    return 0.8 - 0.6 * math.exp(-0.3 * layer)


def _t5_bucket_np(rel):
    half = N_BUCKETS // 2
    max_exact = half // 2
    n = np.abs(rel)
    nf = np.maximum(n, 1).astype(np.float32)
    large = max_exact + (np.log(nf / np.float32(max_exact)) / np.float32(math.log(MAX_DISTANCE / max_exact))
                         * np.float32(half - max_exact)).astype(np.int32)
    large = np.minimum(large, half - 1)
    return np.where(rel > 0, half, 0) + np.where(n < max_exact, n, large)


def _bias_tile(table, q_pos, k_pos, valid):
    b = _t5_bucket_np(k_pos[None, :] - q_pos[:, None])
    t = jnp.moveaxis(table.astype(F32)[jnp.asarray(b)], -1, 0)
    return jnp.where(jnp.asarray(valid)[None], t, MASKED)


def _mod_spec(mod, tm, tn, imap_rows, imap_one):
    if mod.shape[1] == 1:
        return pl.BlockSpec((None, 1, tn), imap_one)
    return pl.BlockSpec((None, tm, tn), imap_rows)


def _ada_kernel(c_ref, w_ref, b_ref, o_ref):
    c = c_ref[...]
    a = (c / (1.0 + jnp.exp(-c))).astype(BF16)
    o_ref[...] = jnp.dot(a, w_ref[...].astype(BF16), preferred_element_type=F32) + b_ref[...]


def _ada_mod(c_all, w_ada, b_ada):
    rows = c_all.shape[0]
    depth, d, n = w_ada.shape
    tn = 1024
    return pl.pallas_call(
        _ada_kernel,
        grid=(depth, n // tn),
        in_specs=[pl.BlockSpec((rows, d), lambda l, j: (0, 0)),
                  pl.BlockSpec((None, d, tn), lambda l, j: (l, 0, j)),
                  pl.BlockSpec((None, 1, tn), lambda l, j: (l, 0, j))],
        out_specs=pl.BlockSpec((None, rows, tn), lambda l, j: (l, 0, j)),
        out_shape=jax.ShapeDtypeStruct((depth, rows, n), F32),
        compiler_params=_cparams("parallel", "parallel"),
        name="ada_mod",
    )(c_all, w_ada, b_ada.reshape(depth, 1, n))


def _norm_mod_f32(x, g, sc, sh):
    y = x * lax.rsqrt(jnp.mean(x * x, axis=-1, keepdims=True) + EPS) * g
    return y * (1.0 + sc) + sh


def _norm_mod_kernel(x_ref, g_ref, sc_ref, sh_ref, o_ref):
    o_ref[...] = _norm_mod_f32(x_ref[...], g_ref[...], sc_ref[...], sh_ref[...]).astype(o_ref.dtype)


def _norm_mod(x, gain, sc, sh):
    b, s, d = x.shape
    tm = min(512, s)
    rows = lambda bb, i: (bb, i, 0)
    one = lambda bb, i: (bb, 0, 0)
    return pl.pallas_call(
        _norm_mod_kernel,
        grid=(b, s // tm),
        in_specs=[pl.BlockSpec((None, tm, d), rows),
                  pl.BlockSpec((1, d), lambda bb, i: (0, 0)),
                  _mod_spec(sc, tm, d, rows, one),
                  _mod_spec(sh, tm, d, rows, one)],
        out_specs=pl.BlockSpec((None, tm, d), rows),
        out_shape=jax.ShapeDtypeStruct((b, s, d), BF16),
        compiler_params=_cparams("parallel", "parallel"),
        name="norm_mod",
    )(x, gain.reshape(1, d), sc, sh)


def _proj_kernel(*refs, has_gain, scale, out32, outbf):
    x_ref, w_ref = refs[0], refs[1]
    pos = 2
    g_ref = None
    if has_gain:
        g_ref = refs[pos]
        pos += 1
    o32_ref = obf_ref = None
    if out32:
        o32_ref = refs[pos]
        pos += 1
    if outbf:
        obf_ref = refs[pos]
    acc = jnp.dot(x_ref[...], w_ref[...], preferred_element_type=F32)
    tn = acc.shape[1]
    for c in range(tn // HEAD_DIM):
        sl = slice(c * HEAD_DIM, (c + 1) * HEAD_DIM)
        t = acc[:, sl]
        if has_gain:
            t = t * lax.rsqrt(jnp.mean(t * t, axis=-1, keepdims=True) + EPS) * g_ref[...]
        if out32:
            o32_ref[:, sl] = t
        if outbf:
            obf_ref[:, sl] = (t * scale).astype(BF16) if scale != 1.0 else t.astype(BF16)


def _proj(x, w, gain=None, scale=1.0, out32=False, outbf=True):
    b, s, k = x.shape
    n = w.shape[1]
    tm = min(1024, s)
    tn = min(1024, n)
    in_specs = [pl.BlockSpec((None, tm, k), lambda j, bb, i: (bb, i, 0)),
                pl.BlockSpec((k, tn), lambda j, bb, i: (0, j))]
    args = [x, w]
    if gain is not None:
        in_specs.append(pl.BlockSpec((1, HEAD_DIM), lambda j, bb, i: (0, 0)))
        args.append(gain.reshape(1, HEAD_DIM).astype(F32))
    out_specs, out_shape = [], []
    for flag, dt in ((out32, F32), (outbf, BF16)):
        if flag:
            out_specs.append(pl.BlockSpec((None, tm, tn), lambda j, bb, i: (bb, i, j)))
            out_shape.append(jax.ShapeDtypeStruct((b, s, n), dt))
    outs = pl.pallas_call(
        functools.partial(_proj_kernel, has_gain=gain is not None, scale=scale, out32=out32, outbf=outbf),
        grid=(n // tn, b, s // tm),
        in_specs=in_specs, out_specs=out_specs, out_shape=out_shape,
        compiler_params=_cparams("parallel", "parallel", "parallel"),
        name="proj",
    )(*args)
    return outs if len(outs) > 1 else outs[0]


def _out_res_kernel(o_ref, w_ref, x_ref, g_ref, y_ref):
    acc = jnp.dot(o_ref[...], w_ref[...], preferred_element_type=F32)
    y_ref[...] = x_ref[...] + g_ref[...] * acc


def _out_res(o, w, x, gate):
    b, s, k = o.shape
    n = w.shape[1]
    tm = min(1024, s)
    tn = min(1024, n)
    rows = lambda j, bb, i: (bb, i, j)
    one = lambda j, bb, i: (bb, 0, j)
    return pl.pallas_call(
        _out_res_kernel,
        grid=(n // tn, b, s // tm),
        in_specs=[pl.BlockSpec((None, tm, k), lambda j, bb, i: (bb, i, 0)),
                  pl.BlockSpec((k, tn), lambda j, bb, i: (0, j)),
                  pl.BlockSpec((None, tm, tn), rows),
                  _mod_spec(gate, tm, tn, rows, one)],
        out_specs=pl.BlockSpec((None, tm, tn), rows),
        out_shape=jax.ShapeDtypeStruct((b, s, n), F32),
        compiler_params=_cparams("parallel", "parallel", "parallel"),
        name="out_res",
    )(o, w, x, gate)


def _sink_attend(q, k_parts, v_parts, bias_parts, sink):
    s_parts = [lax.dot_general(q, kp, NT_DIMS, preferred_element_type=F32) + bp
               for kp, bp in zip(k_parts, bias_parts)]
    m = sink
    for sp in s_parts:
        m = jnp.maximum(m, jnp.max(sp, axis=-1, keepdims=True))
    denom = jnp.exp(sink - m)
    e_parts = []
    for sp in s_parts:
        e = jnp.exp(sp - m)
        denom = denom + jnp.sum(e, axis=-1, keepdims=True)
        e_parts.append(e)
    inv = 1.0 / denom
    o = None
    for e, vp in zip(e_parts, v_parts):
        t = jnp.dot((e * inv).astype(BF16), vp, preferred_element_type=F32)
        o = t if o is None else o + t
    return o


def _attn_a_prompt_kernel(q_ref, kp_ref, k_ref, vp_ref, v_ref, bias_ref, bias0_ref, bias1_ref, sink_ref, o_ref,
                          kc_sc, vc_sc, s_sc, p_sc, *, tq):
    i = pl.program_id(2)
    prev = WINDOW_CHUNKS * CHUNK
    kc_sc[:prev] = kp_ref[...]
    kc_sc[prev:] = k_ref[...]
    vc_sc[:prev] = vp_ref[...]
    vc_sc[prev:] = v_ref[...]
    first = i == 0
    n_chunks = tq // CHUNK

    def scores(c):
        rs = slice(c * CHUNK, (c + 1) * CHUNK)
        qc = jnp.concatenate([q_ref[rs, g * HEAD_DIM:(g + 1) * HEAD_DIM] for g in range(A_GROUP)], axis=0)
        s_sc[c] = lax.dot_general(qc, kc_sc[c * CHUNK:c * CHUNK + BAND], NT_DIMS, preferred_element_type=F32)

    def softmax(c):
        bias = bias_ref[...]
        if c == 0:
            bias = jnp.where(first, bias0_ref[...], bias)
        elif c == 1:
            bias = jnp.where(first, bias1_ref[...], bias)
        s = s_sc[c] + bias
        sink = sink_ref[...]
        m = jnp.maximum(sink, jnp.max(s, axis=-1, keepdims=True))
        e = jnp.exp(s - m)
        inv = 1.0 / (jnp.exp(sink - m) + jnp.sum(e, axis=-1, keepdims=True))
        p_sc[c] = (e * inv).astype(BF16)

    def values(c):
        rs = slice(c * CHUNK, (c + 1) * CHUNK)
        o = jnp.dot(p_sc[c], vc_sc[c * CHUNK:c * CHUNK + BAND], preferred_element_type=F32)
        for g in range(A_GROUP):
            o_ref[rs, g * HEAD_DIM:(g + 1) * HEAD_DIM] = o[g * CHUNK:(g + 1) * CHUNK].astype(o_ref.dtype)

    stages = (scores, softmax, values)
    for tick in range(n_chunks + len(stages) - 1):
        for si, stage in enumerate(stages):
            if 0 <= tick - si < n_chunks:
                stage(tick - si)


def _attn_a_prompt(q, k, v, table, sinks):
    b, s, _ = q.shape
    tq = min(512, s)
    prev = WINDOW_CHUNKS * CHUNK
    r = tq // prev
    k_loc = np.arange(BAND)
    q_loc = prev + np.arange(CHUNK)

    def tiles(first_key):
        valid = np.broadcast_to((k_loc >= first_key)[None, :], (CHUNK, BAND))
        t = _bias_tile(table, q_loc, k_loc, valid)
        return t.reshape(A_KV_HEADS, A_GROUP * CHUNK, BAND)

    bias, bias0, bias1 = tiles(0), tiles(prev), tiles(CHUNK)
    sink_col = jnp.repeat(sinks.astype(F32).reshape(A_KV_HEADS, A_GROUP), CHUNK, axis=1)[..., None]
    qmap = lambda bb, h, i: (bb, i, h)
    pmap = lambda bb, h, i: (bb, jnp.maximum(i * r - 1, 0), h)
    bmap = lambda bb, h, i: (h, 0, 0)
    bspec = pl.BlockSpec((None, A_GROUP * CHUNK, BAND), bmap)
    return pl.pallas_call(
        functools.partial(_attn_a_prompt_kernel, tq=tq),
        grid=(b, A_KV_HEADS, s // tq),
        in_specs=[pl.BlockSpec((None, tq, A_GROUP * HEAD_DIM), qmap),
                  pl.BlockSpec((None, prev, HEAD_DIM), pmap),
                  pl.BlockSpec((None, tq, HEAD_DIM), qmap),
                  pl.BlockSpec((None, prev, HEAD_DIM), pmap),
                  pl.BlockSpec((None, tq, HEAD_DIM), qmap),
                  bspec, bspec, bspec,
                  pl.BlockSpec((None, A_GROUP * CHUNK, 1), bmap)],
        out_specs=pl.BlockSpec((None, tq, A_GROUP * HEAD_DIM), qmap),
        out_shape=jax.ShapeDtypeStruct((b, s, D_MODEL), BF16),
        scratch_shapes=[pltpu.VMEM((prev + tq, HEAD_DIM), BF16), pltpu.VMEM((prev + tq, HEAD_DIM), BF16),
                        pltpu.VMEM((tq // CHUNK, A_GROUP * CHUNK, BAND), F32),
                        pltpu.VMEM((tq // CHUNK, A_GROUP * CHUNK, BAND), BF16)],
        compiler_params=_cparams("parallel", "parallel", "parallel"),
        name="attn_a_prompt",
    )(q, k, k, v, v, bias, bias0, bias1, sink_col)


def _attn_a_sample_kernel(q_ref, kc_ref, kn_ref, vc_ref, vn_ref, bc_ref, bn_ref, sink_ref, o_ref, *, t):
    qs = jnp.concatenate([q_ref[:, g * HEAD_DIM:(g + 1) * HEAD_DIM] for g in range(A_GROUP)], axis=0)
    o = _sink_attend(qs,
                     [kc_ref[...].astype(BF16), kn_ref[...].astype(BF16)],
                     [vc_ref[...].astype(BF16), vn_ref[...].astype(BF16)],
                     [bc_ref[...], bn_ref[...]], sink_ref[...])
    for g in range(A_GROUP):
        o_ref[:, g * HEAD_DIM:(g + 1) * HEAD_DIM] = o[g * t:(g + 1) * t].astype(o_ref.dtype)


def _attn_a_sample(q, ck, cv, kn, vn, table, sinks, past_len):
    b, t, _ = q.shape
    cache = ck.shape[1]
    q_pos = past_len + np.arange(t)
    k_pos = np.concatenate([past_len - cache + np.arange(cache), q_pos])
    qc, kc = q_pos[:, None] // CHUNK, k_pos[None, :] // CHUNK
    valid = (kc <= qc) & (kc >= qc - WINDOW_CHUNKS)
    bias = _bias_tile(table, q_pos, k_pos, valid).reshape(A_KV_HEADS, A_GROUP * t, cache + t)
    sink_col = jnp.repeat(sinks.astype(F32).reshape(A_KV_HEADS, A_GROUP), t, axis=1)[..., None]
    hmap = lambda bb, h: (bb, 0, h)
    bmap = lambda bb, h: (h, 0, 0)
    return pl.pallas_call(
        functools.partial(_attn_a_sample_kernel, t=t),
        grid=(b, A_KV_HEADS),
        in_specs=[pl.BlockSpec((None, t, A_GROUP * HEAD_DIM), hmap),
                  pl.BlockSpec((None, cache, HEAD_DIM), hmap),
                  pl.BlockSpec((None, t, HEAD_DIM), hmap),
                  pl.BlockSpec((None, cache, HEAD_DIM), hmap),
                  pl.BlockSpec((None, t, HEAD_DIM), hmap),
                  pl.BlockSpec((None, A_GROUP * t, cache), bmap),
                  pl.BlockSpec((None, A_GROUP * t, t), bmap),
                  pl.BlockSpec((None, A_GROUP * t, 1), bmap)],
        out_specs=pl.BlockSpec((None, t, A_GROUP * HEAD_DIM), hmap),
        out_shape=jax.ShapeDtypeStruct((b, t, D_MODEL), BF16),
        compiler_params=_cparams("parallel", "parallel"),
        name="attn_a_sample",
    )(q, ck, kn, cv, vn, bias[..., :cache], bias[..., cache:], sink_col)


def _diff_lambda(lam_ref, lam_init):
    lp = lam_ref[...]
    a = jnp.sum(lp[0:1] * lp[1:2], axis=-1, keepdims=True)
    c = jnp.sum(lp[2:3] * lp[3:4], axis=-1, keepdims=True)
    return jnp.exp(a) - jnp.exp(c) + lam_init


def _finish_b(o0, o1, lam, gsub, lam_init):
    o = o0 - lam * o1
    o = o * lax.rsqrt(jnp.mean(o * o, axis=-1, keepdims=True) + EPS) * gsub
    return o * (1.0 - lam_init)


def _attn_b_prompt_kernel(q_ref, k_ref, v_ref, bias_ref, lam_ref, gsub_ref, o_ref,
                          s0_sc, s1_sc, p0_sc, p1_sc, al_sc, m_sc, l_sc, acc_sc, *, t, lam_init):
    i = pl.program_id(2)
    s_scs = (s0_sc, s1_sc)
    p_scs = (p0_sc, p1_sc)
    m_sc[...] = jnp.full(m_sc.shape, MASKED, F32)
    l_sc[...] = jnp.zeros(l_sc.shape, F32)
    acc_sc[...] = jnp.zeros(acc_sc.shape, F32)

    def keys(j):
        return pl.ds(pl.multiple_of(j * t, t), t)

    def scores(j, mp):
        sl = slice(mp * HEAD_DIM, (mp + 1) * HEAD_DIM)
        s_scs[mp][...] = lax.dot_general(q_ref[:, sl], k_ref[keys(j), sl], NT_DIMS, preferred_element_type=F32)

    def softmax(j, mp):
        kind = jnp.minimum(i - j, 2)
        s = s_scs[mp][...] + bias_ref[mp, kind]
        m_old = m_sc[mp]
        m_new = jnp.maximum(m_old, jnp.max(s, axis=-1, keepdims=True))
        alpha = jnp.exp2(m_old - m_new)
        p = jnp.exp2(s - m_new)
        l_sc[mp] = alpha * l_sc[mp] + jnp.sum(p, axis=-1, keepdims=True)
        m_sc[mp] = m_new
        al_sc[mp] = alpha
        p_scs[mp][...] = p.astype(BF16)

    def weigh(j, mp):
        acc_sc[mp] = al_sc[mp] * acc_sc[mp] + jnp.dot(p_scs[mp][...], v_ref[keys(j), :],
                                                      preferred_element_type=F32)

    scores(0, 0)
    scores(0, 1)
    softmax(0, 0)

    def body(j, carry):
        scores(j, 0)
        softmax(j - 1, 1)
        weigh(j - 1, 0)
        scores(j, 1)
        softmax(j, 0)
        weigh(j - 1, 1)
        return carry

    lax.fori_loop(1, i + 1, body, 0)
    softmax(i, 1)
    weigh(i, 0)
    weigh(i, 1)
    lam = _diff_lambda(lam_ref, lam_init)
    o = _finish_b(acc_sc[0] / l_sc[0], acc_sc[1] / l_sc[1], lam, gsub_ref[...], lam_init)
    o_ref[...] = o.astype(o_ref.dtype)


def _toeplitz(f, t):
    h = f.shape[0]
    g = jnp.pad(f, ((0, 0), (0, 1)))
    flat = jnp.tile(g, (1, t))[:, :t * (2 * t - 1)]
    return flat.reshape(h, t, 2 * t - 1)[:, :, t - 1:]


def _attn_b_prompt(q, k, v, table, lam_rows, gsub, lam_init):
    b, s, _ = q.shape
    t = min(512, s)
    loc = np.arange(t)
    chunk_ok = jnp.asarray((loc[None, :] // CHUNK) <= (loc[:, None] // CHUNK))
    tab = table.astype(F32) * LOG2E
    rel = np.arange(-(t - 1), t)
    diag = _toeplitz(tab[jnp.asarray(_t5_bucket_np(rel))].T, t)
    diag = jnp.where(chunk_ok[None], diag, MASKED)
    sub = _toeplitz(tab[jnp.asarray(_t5_bucket_np(rel - t))].T, t)
    far_rel = -(t + 1 + np.arange(max(s - t, 1)))
    far_bucket = _t5_bucket_np(far_rel)
    assert np.all(far_bucket == far_bucket[0])
    far = jnp.broadcast_to(tab[int(far_bucket[0])][:, None, None], (2 * B_HEADS, t, t))
    bias = jnp.stack([diag, sub, far], axis=1)
    qmap = lambda bb, h, i: (bb, i, h)
    kmap = lambda bb, h, i: (bb, 0, h)
    return pl.pallas_call(
        functools.partial(_attn_b_prompt_kernel, t=t, lam_init=lam_init),
        grid=(b, B_HEADS, s // t),
        in_specs=[pl.BlockSpec((None, t, B_VDIM), qmap),
                  pl.BlockSpec((None, s, B_VDIM), kmap),
                  pl.BlockSpec((None, s, B_VDIM), kmap),
                  pl.BlockSpec((2, 3, t, t), lambda bb, h, i: (h, 0, 0, 0)),
                  pl.BlockSpec((4, HEAD_DIM), lambda bb, h, i: (0, 0)),
                  pl.BlockSpec((1, B_VDIM), lambda bb, h, i: (0, 0))],
        out_specs=pl.BlockSpec((None, t, B_VDIM), qmap),
        out_shape=jax.ShapeDtypeStruct((b, s, D_MODEL), BF16),
        scratch_shapes=[pltpu.VMEM((t, t), F32), pltpu.VMEM((t, t), F32),
                        pltpu.VMEM((t, t), BF16), pltpu.VMEM((t, t), BF16),
                        pltpu.VMEM((2, t, 1), F32), pltpu.VMEM((2, t, 1), F32), pltpu.VMEM((2, t, 1), F32),
                        pltpu.VMEM((2, t, B_VDIM), F32)],
        compiler_params=_cparams("parallel", "parallel", "parallel"),
        name="attn_b_prompt",
    )(q, k, v, bias, lam_rows, gsub.reshape(1, B_VDIM).astype(F32))


def _attn_b_sample_kernel(q_ref, kc_ref, kn_ref, vc_ref, vn_ref, bc_ref, bn_ref, lam_ref, gsub_ref, o_ref,
                          *, lam_init):
    kc = kc_ref[...].astype(BF16)
    kn = kn_ref[...].astype(BF16)
    vc = vc_ref[...].astype(BF16)
    vn = vn_ref[...].astype(BF16)
    outs = []
    for mp in range(2):
        sl = slice(mp * HEAD_DIM, (mp + 1) * HEAD_DIM)
        qm = q_ref[:, sl]
        sc = lax.dot_general(qm, kc[:, sl], NT_DIMS, preferred_element_type=F32) + bc_ref[mp]
        sn = lax.dot_general(qm, kn[:, sl], NT_DIMS, preferred_element_type=F32) + bn_ref[mp]
        m = jnp.maximum(jnp.max(sc, axis=-1, keepdims=True), jnp.max(sn, axis=-1, keepdims=True))
        pc = jnp.exp(sc - m)
        pn = jnp.exp(sn - m)
        inv = 1.0 / (jnp.sum(pc, axis=-1, keepdims=True) + jnp.sum(pn, axis=-1, keepdims=True))
        outs.append(jnp.dot((pc * inv).astype(BF16), vc, preferred_element_type=F32)
                    + jnp.dot((pn * inv).astype(BF16), vn, preferred_element_type=F32))
    lam = _diff_lambda(lam_ref, lam_init)
    o_ref[...] = _finish_b(outs[0], outs[1], lam, gsub_ref[...], lam_init).astype(o_ref.dtype)


def _attn_b_sample(q, ck, cv, kn, vn, table, lam_rows, gsub, lam_init, past_len):
    b, t, _ = q.shape
    past = ck.shape[1]
    q_pos = past_len + np.arange(t)
    k_pos = np.arange(past_len + t)
    valid = (k_pos[None, :] // CHUNK) <= (q_pos[:, None] // CHUNK)
    bias = _bias_tile(table, q_pos, k_pos, valid).reshape(B_HEADS, 2, t, past + t)
    hmap = lambda bb, h: (bb, 0, h)
    bmap = lambda bb, h: (h, 0, 0, 0)
    return pl.pallas_call(
        functools.partial(_attn_b_sample_kernel, lam_init=lam_init),
        grid=(b, B_HEADS),
        in_specs=[pl.BlockSpec((None, t, B_VDIM), hmap),
                  pl.BlockSpec((None, past, B_VDIM), hmap),
                  pl.BlockSpec((None, t, B_VDIM), hmap),
                  pl.BlockSpec((None, past, B_VDIM), hmap),
                  pl.BlockSpec((None, t, B_VDIM), hmap),
                  pl.BlockSpec((None, 2, t, past), bmap),
                  pl.BlockSpec((None, 2, t, t), bmap),
                  pl.BlockSpec((4, HEAD_DIM), lambda bb, h: (0, 0)),
                  pl.BlockSpec((1, B_VDIM), lambda bb, h: (0, 0))],
        out_specs=pl.BlockSpec((None, t, B_VDIM), hmap),
        out_shape=jax.ShapeDtypeStruct((b, t, D_MODEL), BF16),
        compiler_params=_cparams("parallel", "parallel"),
        name="attn_b_sample",
    )(q, ck, kn, cv, vn, bias[..., :past], bias[..., past:], lam_rows, gsub.reshape(1, B_VDIM).astype(F32))


def _neg_abs(x):
    bits = lax.bitcast_convert_type(x, jnp.uint32) | jnp.uint32(0x80000000)
    return lax.bitcast_convert_type(bits, F32)


def _minus_later_ones(n):
    idx = np.arange(n)
    return jnp.asarray(-(idx[:, None] > idx[None, :]).astype(np.float32), dtype=BF16)


def _sb_block(q, kb, vb, u, run, mask):
    z = lax.dot_general(q, kb, NT_DIMS, preferred_element_type=F32)
    sp = jnp.maximum(z, 0.0) + jnp.log(1.0 + jnp.exp(-jnp.abs(z)))
    if mask is not None:
        sp = jnp.where(mask, sp, 0.0)
    after = jnp.dot(sp.astype(BF16), u, preferred_element_type=F32) + run
    a = jnp.exp((z - sp) + after)
    if mask is not None:
        a = jnp.where(mask, a, 0.0)
    o = jnp.dot(a.astype(BF16), vb, preferred_element_type=F32)
    return o, run - jnp.sum(sp, axis=-1, keepdims=True)


C_HEADS_PER_STEP = 4


def _attn_c_prompt_kernel(q_ref, k_ref, v_ref, u_ref, o_ref, z_sc, spb_sc, aft_sc, a_sc, off_sc, run_sc, acc_sc,
                          *, t):
    i = pl.program_id(2)
    hp = C_HEADS_PER_STEP
    heads = [slice(h * HEAD_DIM, (h + 1) * HEAD_DIM) for h in range(hp)]
    run_sc[...] = jnp.zeros(run_sc.shape, F32)
    acc_sc[...] = jnp.zeros(acc_sc.shape, F32)

    def block(start, masked):
        keys = pl.ds(start, t)
        if masked:
            visible = lax.broadcasted_iota(jnp.int32, (t, t), 1) < lax.broadcasted_iota(jnp.int32, (t, t), 0)

        def scores(h):
            z_sc[h] = lax.dot_general(q_ref[:, heads[h]], k_ref[keys, heads[h]], NT_DIMS,
                                      preferred_element_type=F32)

        def softplus(h):
            z = z_sc[h]
            sp = jnp.maximum(z, 0.0) + jnp.log(1.0 + jnp.exp(_neg_abs(z)))
            if masked:
                sp = jnp.where(visible, sp, 0.0)
            z_sc[h] = z - sp
            spb_sc[h] = sp.astype(BF16)
            run = run_sc[h]
            off_sc[h] = run
            run_sc[h] = run - jnp.sum(sp, axis=-1, keepdims=True)

        def later_sums(h):
            aft_sc[h] = jnp.dot(spb_sc[h], u_ref[...], preferred_element_type=F32)

        def weights(h):
            a = jnp.exp(z_sc[h] + aft_sc[h] + off_sc[h])
            if masked:
                a = jnp.where(visible, a, 0.0)
            a_sc[h] = a.astype(BF16)

        def values(h):
            acc_sc[h] += jnp.dot(a_sc[h], v_ref[keys, heads[h]], preferred_element_type=F32)

        stages = (scores, softplus, later_sums, weights, values)
        for tick in range(hp + len(stages) - 1):
            for si, stage in enumerate(stages):
                if 0 <= tick - si < hp:
                    stage(tick - si)

    block(pl.multiple_of(i * t, t), True)

    def body(n, carry):
        block(pl.multiple_of((i - 1 - n) * t, t), False)
        return carry

    lax.fori_loop(0, i, body, 0)
    for h in range(hp):
        o_ref[:, heads[h]] = acc_sc[h].astype(o_ref.dtype)


def _attn_c_prompt(q, k, v):
    b, s, _ = q.shape
    t = min(256, s)
    hp = C_HEADS_PER_STEP
    qmap = lambda bb, h, i: (bb, i, h)
    kmap = lambda bb, h, i: (bb, 0, h)
    return pl.pallas_call(
        functools.partial(_attn_c_prompt_kernel, t=t),
        grid=(b, C_HEADS // hp, s // t),
        in_specs=[pl.BlockSpec((None, t, hp * HEAD_DIM), qmap),
                  pl.BlockSpec((None, s, hp * HEAD_DIM), kmap),
                  pl.BlockSpec((None, s, hp * HEAD_DIM), kmap),
                  pl.BlockSpec((t, t), lambda bb, h, i: (0, 0))],
        out_specs=pl.BlockSpec((None, t, hp * HEAD_DIM), qmap),
        out_shape=jax.ShapeDtypeStruct((b, s, D_MODEL), BF16),
        scratch_shapes=[pltpu.VMEM((hp, t, t), F32), pltpu.VMEM((hp, t, t), BF16), pltpu.VMEM((hp, t, t), F32),
                        pltpu.VMEM((hp, t, t), BF16), pltpu.VMEM((hp, t, 1), F32),
                        pltpu.VMEM((hp, t, 1), F32), pltpu.VMEM((hp, t, HEAD_DIM), F32)],
        compiler_params=_cparams("parallel", "parallel", "parallel"),
        name="attn_c_prompt",
    )(q, k, v, _minus_later_ones(t))


def _attn_c_sample_kernel(q_ref, kc_ref, kn_ref, vc_ref, vn_ref, u_ref, o_ref, *, t, past, tk):
    q = q_ref[...]
    u = u_ref[...]
    row = lax.broadcasted_iota(jnp.int32, (t, t), 0)
    col = lax.broadcasted_iota(jnp.int32, (t, t), 1)
    acc, run = _sb_block(q, kn_ref[...].astype(BF16), vn_ref[...].astype(BF16), u[:t, :t],
                         jnp.zeros((t, 1), F32), col < row)
    for j in reversed(range(past // tk)):
        rs = slice(j * tk, (j + 1) * tk)
        o, run = _sb_block(q, kc_ref[rs, :].astype(BF16), vc_ref[rs, :].astype(BF16), u, run, None)
        acc = acc + o
    o_ref[...] = acc.astype(o_ref.dtype)


def _attn_c_sample(q, ck, cv, kn, vn):
    b, t, _ = q.shape
    past = ck.shape[1]
    tk = min(256, past)
    hmap = lambda bb, h: (bb, 0, h)
    return pl.pallas_call(
        functools.partial(_attn_c_sample_kernel, t=t, past=past, tk=tk),
        grid=(b, C_HEADS),
        in_specs=[pl.BlockSpec((None, t, HEAD_DIM), hmap),
                  pl.BlockSpec((None, past, HEAD_DIM), hmap),
                  pl.BlockSpec((None, t, HEAD_DIM), hmap),
                  pl.BlockSpec((None, past, HEAD_DIM), hmap),
                  pl.BlockSpec((None, t, HEAD_DIM), hmap),
                  pl.BlockSpec((tk, tk), lambda bb, h: (0, 0))],
        out_specs=pl.BlockSpec((None, t, HEAD_DIM), hmap),
        out_shape=jax.ShapeDtypeStruct((b, t, D_MODEL), BF16),
        compiler_params=_cparams("parallel", "parallel"),
        name="attn_c_sample",
    )(q, ck, kn, cv, vn, _minus_later_ones(tk))


ROUTE_LANES = LANES
EXPERT_LANE0 = N_GROUPS
ROW_LANES = LANES
ROW_TILE = D_MODEL // ROW_LANES


def _store_row_tiled(ref, val):
    rows = val.shape[0]
    for s in range(ROW_TILE):
        ref[pl.ds(s, rows, stride=ROW_TILE), :] = val[:, s * ROW_LANES:(s + 1) * ROW_LANES]


def _load_row_tiled(ref, rows, s, base=0):
    return ref[pl.ds(base * ROW_TILE + s, rows, stride=ROW_TILE), :]


def _router_kernel(x_ref, g_ref, sc_ref, sh_ref, w_ref, b_ref, h_ref, r_ref):
    h = _norm_mod_f32(x_ref[...], g_ref[...], sc_ref[...], sh_ref[...])
    _store_row_tiled(h_ref, h)
    h_hi = h.astype(BF16)
    h_lo = (h - h_hi.astype(F32)).astype(BF16)
    w = w_ref[...]
    both = jnp.dot(h_hi, w, preferred_element_type=F32)
    logits = (both[:, :ROUTE_LANES] + both[:, ROUTE_LANES:]
              + jnp.dot(h_lo, w[:, :ROUTE_LANES], preferred_element_type=F32) + b_ref[...])
    lane = lax.broadcasted_iota(jnp.int32, logits.shape, 1)
    lanef = lane.astype(F32)

    def first_lane_of_max(v, vmax):
        return jnp.min(jnp.where(v == vmax, lanef, float(ROUTE_LANES)), axis=-1, keepdims=True)

    lg = jnp.where(lane < N_GROUPS, logits, MASKED)
    mg = jnp.max(lg, axis=-1, keepdims=True)
    gate = 1.0 / jnp.sum(jnp.exp(lg - mg), axis=-1, keepdims=True)
    gi = first_lane_of_max(lg, mg)
    lo = EXPERT_LANE0 + EXPERTS_PER_GROUP * gi
    le = jnp.where((lanef >= lo) & (lanef < lo + EXPERTS_PER_GROUP), logits, MASKED)
    v1 = jnp.max(le, axis=-1, keepdims=True)
    i1 = first_lane_of_max(le, v1)
    le2 = jnp.where(lanef == i1, MASKED, le)
    v2 = jnp.max(le2, axis=-1, keepdims=True)
    i2 = first_lane_of_max(le2, v2)
    e21 = jnp.exp(v2 - v1)
    w1 = gate / (1.0 + e21)
    w2 = w1 * e21
    r_ref[...] = jnp.where(lane == 0, i1 - EXPERT_LANE0,
                           jnp.where(lane == 1, i2 - EXPERT_LANE0,
                                     jnp.where(lane == 2, w1, jnp.where(lane == 3, w2, 0.0))))


def _router(x, gain, sc, sh, w_cat, b_row):
    b, s, d = x.shape
    tm = min(512, s)
    rows = lambda bb, i: (bb, i, 0)
    one = lambda bb, i: (bb, 0, 0)
    return pl.pallas_call(
        _router_kernel,
        grid=(b, s // tm),
        in_specs=[pl.BlockSpec((None, tm, d), rows),
                  pl.BlockSpec((1, d), lambda bb, i: (0, 0)),
                  _mod_spec(sc, tm, d, rows, one),
                  _mod_spec(sh, tm, d, rows, one),
                  pl.BlockSpec((d, 2 * ROUTE_LANES), lambda bb, i: (0, 0)),
                  pl.BlockSpec((1, ROUTE_LANES), lambda bb, i: (0, 0))],
        out_specs=[pl.BlockSpec((None, tm * ROW_TILE, ROW_LANES), rows),
                   pl.BlockSpec((None, tm, ROUTE_LANES), rows)],
        out_shape=[jax.ShapeDtypeStruct((b, s * ROW_TILE, ROW_LANES), F32),
                   jax.ShapeDtypeStruct((b, s, ROUTE_LANES), F32)],
        compiler_params=_cparams("parallel", "parallel"),
        name="moe_router",
    )(x, gain.reshape(1, d), sc, sh, w_cat, b_row)


def _router_weights(w_group, b_group, w_router, b_router):
    w = jnp.concatenate([w_group, jnp.moveaxis(w_router, 0, 1).reshape(D_MODEL, N_EXPERTS)], axis=1)
    w = jnp.pad(w.astype(F32), ((0, 0), (0, ROUTE_LANES - w.shape[1])))
    hi = w.astype(BF16)
    lo = (w - hi.astype(F32)).astype(BF16)
    bias = jnp.concatenate([b_group, b_router.reshape(N_EXPERTS)]).astype(F32)
    bias = jnp.pad(bias, (0, ROUTE_LANES - bias.shape[0])).reshape(1, ROUTE_LANES)
    return jnp.concatenate([hi, lo], axis=1), bias


def _row_gather(src_hbm, idx_ref, n_rows, dst, sem, inline=False):
    def start(r):
        src = pl.multiple_of(idx_ref[0, r] * ROW_TILE, ROW_TILE)
        dst_rows = r * ROW_TILE if isinstance(r, int) else pl.multiple_of(r * ROW_TILE, ROW_TILE)
        pltpu.make_async_copy(src_hbm.at[pl.ds(src, ROW_TILE)], dst.at[pl.ds(dst_rows, ROW_TILE)], sem).start()

    if inline:
        for r in range(n_rows):
            start(r)
    else:
        def body(r, carry):
            start(r)
            return carry
        lax.fori_loop(0, n_rows, body, 0, unroll=8)


def _row_gather_wait(src_hbm, n_rows, dst, sem):
    pltpu.make_async_copy(src_hbm.at[pl.ds(0, n_rows * ROW_TILE)], dst, sem).wait()


def _expert_kernel(te_ref, idx_ref, idx_next_ref, h_hbm, wgu_ref, wd_ref, y_ref, buf, sem, *, tm):
    i = pl.program_id(0)
    n = pl.num_programs(0)
    slot = i % 2

    @pl.when(i == 0)
    def _():
        _row_gather(h_hbm, idx_ref, tm, buf.at[0], sem.at[0])

    _row_gather_wait(h_hbm, tm, buf.at[slot], sem.at[slot])
    _row_gather(h_hbm, idx_next_ref, tm, buf.at[1 - slot], sem.at[1 - slot], inline=True)
    xb = buf.at[slot]
    x = jnp.concatenate([_load_row_tiled(xb, tm, s).astype(BF16) for s in range(ROW_TILE)], axis=1)
    gu = jnp.dot(x, wgu_ref[...], preferred_element_type=F32)
    gate, up = gu[:, :D_EXPERT], gu[:, D_EXPERT:]
    hid = (gate / (1.0 + jnp.exp(-gate)) * up).astype(BF16)
    _store_row_tiled(y_ref, jnp.dot(hid, wd_ref[...], preferred_element_type=F32))

    @pl.when(i == n - 1)
    def _():
        _row_gather_wait(h_hbm, tm, buf.at[1 - slot], sem.at[1 - slot])


def _experts(h_rt, tile_expert, src_rows, w_gu, w_down, tm):
    n_tiles = tile_expert.shape[0]
    d = D_MODEL
    grid_spec = pltpu.PrefetchScalarGridSpec(
        num_scalar_prefetch=1,
        grid=(n_tiles,),
        in_specs=[pl.BlockSpec((None, 1, tm), lambda i, te: (i, 0, 0), memory_space=pltpu.SMEM),
                  pl.BlockSpec((None, 1, tm), lambda i, te: (jnp.minimum(i + 1, n_tiles - 1), 0, 0),
                               memory_space=pltpu.SMEM),
                  pl.BlockSpec(memory_space=pl.ANY),
                  pl.BlockSpec((None, d, 2 * D_EXPERT), lambda i, te: (te[i], 0, 0)),
                  pl.BlockSpec((None, D_EXPERT, d), lambda i, te: (te[i], 0, 0))],
        out_specs=pl.BlockSpec((tm * ROW_TILE, ROW_LANES), lambda i, te: (i, 0)),
        scratch_shapes=[pltpu.VMEM((2, tm * ROW_TILE, ROW_LANES), F32), pltpu.SemaphoreType.DMA((2,))],
    )
    idx3 = src_rows.reshape(n_tiles, 1, tm)
    return pl.pallas_call(
        functools.partial(_expert_kernel, tm=tm),
        grid_spec=grid_spec,
        out_shape=jax.ShapeDtypeStruct((n_tiles * tm * ROW_TILE, ROW_LANES), F32),
        compiler_params=_cparams("arbitrary"),
        name="moe_experts",
    )(tile_expert, idx3, idx3, h_rt, w_gu, w_down)


def _combine_kernel(pos_ref, pos_next_ref, y_hbm, x_ref, g_ref, r_ref, o_ref, buf, sem, *, tc):
    i = pl.program_id(0) * pl.num_programs(1) + pl.program_id(1)
    n = pl.num_programs(0) * pl.num_programs(1)
    slot = i % 2

    @pl.when(i == 0)
    def _():
        _row_gather(y_hbm, pos_ref, 2 * tc, buf.at[0], sem.at[0])

    @pl.when(i + 1 < n)
    def _():
        _row_gather(y_hbm, pos_next_ref, 2 * tc, buf.at[1 - slot], sem.at[1 - slot])

    _row_gather_wait(y_hbm, 2 * tc, buf.at[slot], sem.at[slot])
    yb = buf.at[slot]
    w1 = r_ref[:, 2:3]
    w2 = r_ref[:, 3:4]
    for s in range(ROW_TILE):
        sl = slice(s * ROW_LANES, (s + 1) * ROW_LANES)
        y = w1 * _load_row_tiled(yb, tc, s) + w2 * _load_row_tiled(yb, tc, s, base=tc)
        o_ref[:, sl] = x_ref[:, sl] + g_ref[:, sl] * y


def _combine(x, y_rt, pos, gate, route, tc):
    b, s, d = x.shape
    nt = s // tc
    n = b * nt
    pos3 = pos.reshape(n, tc, 2).transpose(0, 2, 1).reshape(n, 1, 2 * tc)
    rows = lambda bb, i: (bb, i, 0)
    one = lambda bb, i: (bb, 0, 0)
    return pl.pallas_call(
        functools.partial(_combine_kernel, tc=tc),
        grid=(b, nt),
        in_specs=[pl.BlockSpec((None, 1, 2 * tc), lambda bb, i: (bb * nt + i, 0, 0), memory_space=pltpu.SMEM),
                  pl.BlockSpec((None, 1, 2 * tc), lambda bb, i: (jnp.minimum(bb * nt + i + 1, n - 1), 0, 0),
                               memory_space=pltpu.SMEM),
                  pl.BlockSpec(memory_space=pl.ANY),
                  pl.BlockSpec((None, tc, d), rows),
                  _mod_spec(gate, tc, d, rows, one),
                  pl.BlockSpec((None, tc, ROUTE_LANES), rows)],
        out_specs=pl.BlockSpec((None, tc, d), rows),
        out_shape=jax.ShapeDtypeStruct((b, s, d), F32),
        scratch_shapes=[pltpu.VMEM((2, 2 * tc * ROW_TILE, ROW_LANES), F32), pltpu.SemaphoreType.DMA((2,))],
        compiler_params=_cparams("arbitrary", "arbitrary"),
        name="moe_combine",
    )(pos3, pos3, y_rt, x, gate, route)


def _moe(x, gain, sc, sh, gate, wts):
    w_cat, b_row, w_gu, w_down = wts
    b, s, d = x.shape
    tokens = b * s
    h, route = _router(x, gain, sc, sh, w_cat, b_row)
    ids = route.reshape(tokens, ROUTE_LANES)[:, 0:2].astype(jnp.int32).reshape(-1)
    tm = 256 if tokens >= 4096 else 128
    n_assign = 2 * tokens
    n_tiles = (n_assign + N_EXPERTS * (tm - 1)) // tm + 1
    onehot = (ids[:, None] == jnp.arange(N_EXPERTS)[None, :]).astype(jnp.int32)
    csum = jnp.cumsum(onehot, axis=0)
    counts = csum[-1]
    padded = ((counts + tm - 1) // tm) * tm
    ends = jnp.cumsum(padded)
    pos = jnp.sum(onehot * (csum - 1 + (ends - padded)[None, :]), axis=1)
    tile_expert = jnp.minimum(jnp.searchsorted(ends // tm, jnp.arange(n_tiles), side='right'),
                              N_EXPERTS - 1).astype(jnp.int32)
    src_rows = jnp.zeros((n_tiles * tm,), jnp.int32).at[pos].set(jnp.arange(n_assign, dtype=jnp.int32) // 2)
    y = _experts(h.reshape(tokens * ROW_TILE, ROW_LANES), tile_expert, src_rows, w_gu, w_down, tm)
    return _combine(x, y, pos.reshape(tokens, 2), gate, route, min(256, s))


def _split_mod(mod):
    return [mod[..., j * D_MODEL:(j + 1) * D_MODEL] for j in range(6)]


def kernel(x_prompt, x_sample, c_prompt, c_sample, cache_a_k, cache_a_v, cache_b_k, cache_b_v, cache_c_k, cache_c_v, rel_bias_table, norm_mix, norm_ffn, w_ada, b_ada, w_in_a, q_gain_a, k_gain_a, sinks_a, w_out_a, w_in_b, q_gain_b, k_gain_b, lam_q1, lam_k1, lam_q2, lam_k2, sub_gain_b, w_out_b, w_in_c, w_out_c, w_group, b_group, w_router, b_router, w_gate, w_up, w_down):
    nb, seq, d = x_prompt.shape
    db, dt, _ = x_sample.shape
    past_len = cache_b_k.shape[2]
    a_cache = cache_a_k.shape[2]
    ns = db * dt

    n_c = nb + db
    c_rows = -(-n_c // 16) * 16
    c_all = jnp.pad(jnp.concatenate([c_prompt, c_sample], axis=0).astype(F32), ((0, c_rows - n_c), (0, 0)))
    mods = _ada_mod(c_all, w_ada, b_ada)

    xp = x_prompt
    xs = x_sample.reshape(1, ns, d)
    st_p, st_s = [], []
    for l in range(DEPTH):
        i, kind = l // N_MIXERS, l % N_MIXERS
        mp = _split_mod(mods[l, :nb][:, None, :])
        ms = _split_mod(jnp.repeat(mods[l, nb:n_c], dt, axis=0)[None])
        hp = _norm_mod(xp, norm_mix[l], mp[1], mp[0])
        hs = _norm_mod(xs, norm_mix[l], ms[1], ms[0])

        if kind == 0:
            nq, nk = A_HEADS * HEAD_DIM, A_KV_HEADS * HEAD_DIM
            wq = w_in_a[i][:, :nq].astype(BF16)
            wk = w_in_a[i][:, nq:nq + nk].astype(BF16)
            wv = w_in_a[i][:, nq + nk:].astype(BF16)
            wo = w_out_a[i].astype(BF16)
            q = _proj(hp, wq, q_gain_a[i], QK_SCALE)
            k32, kbf = _proj(hp, wk, k_gain_a[i], out32=True)
            v32, vbf = _proj(hp, wv, out32=True)
            o = _attn_a_prompt(q, kbf, vbf, rel_bias_table, sinks_a[i])
            xp = _out_res(o, wo, xp, mp[2])
            st_p.append((k32[:, -a_cache:].reshape(nb, a_cache, A_KV_HEADS, HEAD_DIM),
                         v32[:, -a_cache:].reshape(nb, a_cache, A_KV_HEADS, HEAD_DIM)))
            q = _proj(hs, wq, q_gain_a[i], QK_SCALE).reshape(db, dt, nq)
            kn = _proj(hs, wk, k_gain_a[i], out32=True, outbf=False).reshape(db, dt, nk)
            vn = _proj(hs, wv, out32=True, outbf=False).reshape(db, dt, nk)
            ck = cache_a_k[i].reshape(db, a_cache, nk)
            cv = cache_a_v[i].reshape(db, a_cache, nk)
            o = _attn_a_sample(q, ck, cv, kn, vn, rel_bias_table, sinks_a[i], past_len)
            xs = _out_res(o.reshape(1, ns, d), wo, xs, ms[2])
            kk = jnp.concatenate([ck, kn], axis=1)[:, -a_cache:]
            vv = jnp.concatenate([cv, vn], axis=1)[:, -a_cache:]
            st_s.append((kk.reshape(db, a_cache, A_KV_HEADS, HEAD_DIM),
                         vv.reshape(db, a_cache, A_KV_HEADS, HEAD_DIM)))
        elif kind == 1:
            nq = B_HEADS * 2 * HEAD_DIM
            lam_init = _lambda_init(l)
            wq = w_in_b[i][:, :nq].astype(BF16)
            wk = w_in_b[i][:, nq:2 * nq].astype(BF16)
            wv = w_in_b[i][:, 2 * nq:].astype(BF16)
            wo = w_out_b[i].astype(BF16)
            lam_rows = jnp.stack([lam_q1[i], lam_k1[i], lam_q2[i], lam_k2[i]]).astype(F32)
            q = _proj(hp, wq, q_gain_b[i], QK_SCALE * LOG2E)
            k32, kbf = _proj(hp, wk, k_gain_b[i], out32=True)
            v32, vbf = _proj(hp, wv, out32=True)
            o = _attn_b_prompt(q, kbf, vbf, rel_bias_table, lam_rows, sub_gain_b[i], lam_init)
            xp = _out_res(o, wo, xp, mp[2])
            st_p.append((k32.reshape(nb, seq, B_HEADS, 2, HEAD_DIM), v32.reshape(nb, seq, B_HEADS, B_VDIM)))
            q = _proj(hs, wq, q_gain_b[i], QK_SCALE).reshape(db, dt, nq)
            kn = _proj(hs, wk, k_gain_b[i], out32=True, outbf=False).reshape(db, dt, nq)
            vn = _proj(hs, wv, out32=True, outbf=False).reshape(db, dt, nq)
            o = _attn_b_sample(q, cache_b_k[i].reshape(db, past_len, nq), cache_b_v[i].reshape(db, past_len, nq),
                               kn, vn, rel_bias_table, lam_rows, sub_gain_b[i], lam_init, past_len)
            xs = _out_res(o.reshape(1, ns, d), wo, xs, ms[2])
            st_s.append((kn.reshape(db, dt, B_HEADS, 2, HEAD_DIM), vn.reshape(db, dt, B_HEADS, B_VDIM)))
        else:
            wq = w_in_c[i][:, :d].astype(BF16)
            wk = w_in_c[i][:, d:2 * d].astype(BF16)
            wv = w_in_c[i][:, 2 * d:].astype(BF16)
            wo = w_out_c[i].astype(BF16)
            q = _proj(hp, wq, None, QK_SCALE)
            k32, kbf = _proj(hp, wk, out32=True)
            v32, vbf = _proj(hp, wv, out32=True)
            o = _attn_c_prompt(q, kbf, vbf)
            xp = _out_res(o, wo, xp, mp[2])
            st_p.append((k32.reshape(nb, seq, C_HEADS, HEAD_DIM), v32.reshape(nb, seq, C_HEADS, HEAD_DIM)))
            q = _proj(hs, wq, None, QK_SCALE).reshape(db, dt, d)
            kn = _proj(hs, wk, out32=True, outbf=False).reshape(db, dt, d)
            vn = _proj(hs, wv, out32=True, outbf=False).reshape(db, dt, d)
            o = _attn_c_sample(q, cache_c_k[i].reshape(db, past_len, d), cache_c_v[i].reshape(db, past_len, d),
                               kn, vn)
            xs = _out_res(o.reshape(1, ns, d), wo, xs, ms[2])
            st_s.append((kn.reshape(db, dt, C_HEADS, HEAD_DIM), vn.reshape(db, dt, C_HEADS, HEAD_DIM)))

        w_cat, b_row = _router_weights(w_group[l], b_group[l], w_router[l], b_router[l])
        moe_w = (w_cat, b_row,
                 jnp.concatenate([w_gate[l], w_up[l]], axis=-1).astype(BF16),
                 w_down[l].astype(BF16))
        xp = _moe(xp, norm_ffn[l], mp[4], mp[3], mp[5], moe_w)
        xs = _moe(xs, norm_ffn[l], ms[4], ms[3], ms[5], moe_w)

    def collect(states, kind, j):
        parts = [states[l][j] for l in range(DEPTH) if l % N_MIXERS == kind]
        return parts[0][None] if len(parts) == 1 else jnp.stack(parts)

    return (xp, xs.reshape(db, dt, d),
            collect(st_p, 0, 0), collect(st_p, 0, 1),
            collect(st_p, 1, 0), collect(st_p, 1, 1),
            collect(st_p, 2, 0), collect(st_p, 2, 1),
            collect(st_s, 0, 0), collect(st_s, 0, 1),
            collect(st_s, 1, 0), collect(st_s, 1, 1),
            collect(st_s, 2, 0), collect(st_s, 2, 1))
```

```python
import functools
import math

import numpy as np
import jax
import jax.numpy as jnp
from jax import lax
from jax.experimental import pallas as pl
from jax.experimental.pallas import tpu as pltpu

F32 = jnp.float32
BF16 = jnp.bfloat16

D_MODEL = 2048
DEPTH = 4
CHUNK = 64
HEAD_DIM = 128
N_MIXERS = 3
A_HEADS = 16
A_KV_HEADS = 4
A_GROUP = A_HEADS // A_KV_HEADS
WINDOW_CHUNKS = 2
BAND = (WINDOW_CHUNKS + 1) * CHUNK
B_HEADS = 8
B_VDIM = 2 * HEAD_DIM
C_HEADS = 16
N_BUCKETS = 32
MAX_DISTANCE = 128
N_GROUPS = 4
EXPERTS_PER_GROUP = 4
N_EXPERTS = N_GROUPS * EXPERTS_PER_GROUP
D_EXPERT = D_MODEL // 4
EPS = 1e-6
QK_SCALE = HEAD_DIM ** -0.5
LOG2E = math.log2(math.e)
ROW_CHUNK = 32

LANES = 128
MASKED = -1e30
VMEM_LIMIT = 56 * 1024 * 1024

NT_DIMS = (((1,), (1,)), ((), ()))


def _cparams(*sem):
    return pltpu.CompilerParams(dimension_semantics=sem, vmem_limit_bytes=VMEM_LIMIT)


def _lambda_init(layer):
    return 0.8 - 0.6 * math.exp(-0.3 * layer)


def _t5_bucket_np(rel):
    half = N_BUCKETS // 2
    max_exact = half // 2
    n = np.abs(rel)
    nf = np.maximum(n, 1).astype(np.float32)
    large = max_exact + (np.log(nf / np.float32(max_exact)) / np.float32(math.log(MAX_DISTANCE / max_exact))
                         * np.float32(half - max_exact)).astype(np.int32)
    large = np.minimum(large, half - 1)
    return np.where(rel > 0, half, 0) + np.where(n < max_exact, n, large)


def _bias_tile(table, q_pos, k_pos, valid):
    b = _t5_bucket_np(k_pos[None, :] - q_pos[:, None])
    t = jnp.moveaxis(table.astype(F32)[jnp.asarray(b)], -1, 0)
    return jnp.where(jnp.asarray(valid)[None], t, MASKED)


def _mod_spec(mod, tm, tn, imap_rows, imap_one):
    if mod.shape[1] == 1:
        return pl.BlockSpec((None, 1, tn), imap_one)
    return pl.BlockSpec((None, tm, tn), imap_rows)


def _ada_kernel(c_ref, w_ref, b_ref, o_ref):
    c = c_ref[...]
    a = (c / (1.0 + jnp.exp(-c))).astype(BF16)
    o_ref[...] = jnp.dot(a, w_ref[...].astype(BF16), preferred_element_type=F32) + b_ref[...]


def _ada_mod(c_all, w_ada, b_ada):
    rows = c_all.shape[0]
    depth, d, n = w_ada.shape
    tn = 1024
    return pl.pallas_call(
        _ada_kernel,
        grid=(depth, n // tn),
        in_specs=[pl.BlockSpec((rows, d), lambda l, j: (0, 0)),
                  pl.BlockSpec((None, d, tn), lambda l, j: (l, 0, j)),
                  pl.BlockSpec((None, 1, tn), lambda l, j: (l, 0, j))],
        out_specs=pl.BlockSpec((None, rows, tn), lambda l, j: (l, 0, j)),
        out_shape=jax.ShapeDtypeStruct((depth, rows, n), F32),
        compiler_params=_cparams("parallel", "parallel"),
        name="ada_mod",
    )(c_all, w_ada, b_ada.reshape(depth, 1, n))


def _norm_mod_f32(x, g, sc, sh):
    y = x * lax.rsqrt(jnp.mean(x * x, axis=-1, keepdims=True) + EPS) * g
    return y * (1.0 + sc) + sh


def _norm_mod_kernel(x_ref, g_ref, sc_ref, sh_ref, o_ref):
    o_ref[...] = _norm_mod_f32(x_ref[...], g_ref[...], sc_ref[...], sh_ref[...]).astype(o_ref.dtype)


def _norm_mod(x, gain, sc, sh):
    b, s, d = x.shape
    tm = min(512, s)
    rows = lambda bb, i: (bb, i, 0)
    one = lambda bb, i: (bb, 0, 0)
    return pl.pallas_call(
        _norm_mod_kernel,
        grid=(b, s // tm),
        in_specs=[pl.BlockSpec((None, tm, d), rows),
                  pl.BlockSpec((1, d), lambda bb, i: (0, 0)),
                  _mod_spec(sc, tm, d, rows, one),
                  _mod_spec(sh, tm, d, rows, one)],
        out_specs=pl.BlockSpec((None, tm, d), rows),
        out_shape=jax.ShapeDtypeStruct((b, s, d), BF16),
        compiler_params=_cparams("parallel", "parallel"),
        name="norm_mod",
    )(x, gain.reshape(1, d), sc, sh)


def _proj_kernel(*refs, has_gain, scale, out32, outbf):
    x_ref, w_ref = refs[0], refs[1]
    pos = 2
    g_ref = None
    if has_gain:
        g_ref = refs[pos]
        pos += 1
    o32_ref = obf_ref = None
    if out32:
        o32_ref = refs[pos]
        pos += 1
    if outbf:
        obf_ref = refs[pos]
    acc = jnp.dot(x_ref[...], w_ref[...], preferred_element_type=F32)
    tn = acc.shape[1]
    for c in range(tn // HEAD_DIM):
        sl = slice(c * HEAD_DIM, (c + 1) * HEAD_DIM)
        t = acc[:, sl]
        if has_gain:
            t = t * lax.rsqrt(jnp.mean(t * t, axis=-1, keepdims=True) + EPS) * g_ref[...]
        if out32:
            o32_ref[:, sl] = t
        if outbf:
            obf_ref[:, sl] = (t * scale).astype(BF16) if scale != 1.0 else t.astype(BF16)


def _proj(x, w, gain=None, scale=1.0, out32=False, outbf=True):
    b, s, k = x.shape
    n = w.shape[1]
    tm = min(1024, s)
    tn = min(1024, n)
    in_specs = [pl.BlockSpec((None, tm, k), lambda j, bb, i: (bb, i, 0)),
                pl.BlockSpec((k, tn), lambda j, bb, i: (0, j))]
    args = [x, w]
    if gain is not None:
        in_specs.append(pl.BlockSpec((1, HEAD_DIM), lambda j, bb, i: (0, 0)))
        args.append(gain.reshape(1, HEAD_DIM).astype(F32))
    out_specs, out_shape = [], []
    for flag, dt in ((out32, F32), (outbf, BF16)):
        if flag:
            out_specs.append(pl.BlockSpec((None, tm, tn), lambda j, bb, i: (bb, i, j)))
            out_shape.append(jax.ShapeDtypeStruct((b, s, n), dt))
    outs = pl.pallas_call(
        functools.partial(_proj_kernel, has_gain=gain is not None, scale=scale, out32=out32, outbf=outbf),
        grid=(n // tn, b, s // tm),
        in_specs=in_specs, out_specs=out_specs, out_shape=out_shape,
        compiler_params=_cparams("parallel", "parallel", "parallel"),
        name="proj",
    )(*args)
    return outs if len(outs) > 1 else outs[0]


def _out_res_kernel(o_ref, w_ref, x_ref, g_ref, y_ref):
    acc = jnp.dot(o_ref[...], w_ref[...], preferred_element_type=F32)
    y_ref[...] = x_ref[...] + g_ref[...] * acc


def _out_res(o, w, x, gate):
    b, s, k = o.shape
    n = w.shape[1]
    tm = min(1024, s)
    tn = min(1024, n)
    rows = lambda j, bb, i: (bb, i, j)
    one = lambda j, bb, i: (bb, 0, j)
    return pl.pallas_call(
        _out_res_kernel,
        grid=(n // tn, b, s // tm),
        in_specs=[pl.BlockSpec((None, tm, k), lambda j, bb, i: (bb, i, 0)),
                  pl.BlockSpec((k, tn), lambda j, bb, i: (0, j)),
                  pl.BlockSpec((None, tm, tn), rows),
                  _mod_spec(gate, tm, tn, rows, one)],
        out_specs=pl.BlockSpec((None, tm, tn), rows),
        out_shape=jax.ShapeDtypeStruct((b, s, n), F32),
        compiler_params=_cparams("parallel", "parallel", "parallel"),
        name="out_res",
    )(o, w, x, gate)


def _sink_attend(q, k_parts, v_parts, bias_parts, sink):
    s_parts = [lax.dot_general(q, kp, NT_DIMS, preferred_element_type=F32) + bp
               for kp, bp in zip(k_parts, bias_parts)]
    m = sink
    for sp in s_parts:
        m = jnp.maximum(m, jnp.max(sp, axis=-1, keepdims=True))
    denom = jnp.exp(sink - m)
    e_parts = []
    for sp in s_parts:
        e = jnp.exp(sp - m)
        denom = denom + jnp.sum(e, axis=-1, keepdims=True)
        e_parts.append(e)
    inv = 1.0 / denom
    o = None
    for e, vp in zip(e_parts, v_parts):
        t = jnp.dot((e * inv).astype(BF16), vp, preferred_element_type=F32)
        o = t if o is None else o + t
    return o


def _attn_a_prompt_kernel(q_ref, kp_ref, k_ref, vp_ref, v_ref, bias_ref, bias0_ref, bias1_ref, sink_ref, o_ref,
                          kc_sc, vc_sc, s_sc, p_sc, *, tq):
    i = pl.program_id(2)
    prev = WINDOW_CHUNKS * CHUNK
    kc_sc[:prev] = kp_ref[...]
    kc_sc[prev:] = k_ref[...]
    vc_sc[:prev] = vp_ref[...]
    vc_sc[prev:] = v_ref[...]
    first = i == 0
    n_chunks = tq // CHUNK

    def scores(c):
        rs = slice(c * CHUNK, (c + 1) * CHUNK)
        qc = jnp.concatenate([q_ref[rs, g * HEAD_DIM:(g + 1) * HEAD_DIM] for g in range(A_GROUP)], axis=0)
        s_sc[c] = lax.dot_general(qc, kc_sc[c * CHUNK:c * CHUNK + BAND], NT_DIMS, preferred_element_type=F32)

    def softmax(c):
        bias = bias_ref[...]
        if c == 0:
            bias = jnp.where(first, bias0_ref[...], bias)
        elif c == 1:
            bias = jnp.where(first, bias1_ref[...], bias)
        s = s_sc[c] + bias
        sink = sink_ref[...]
        m = jnp.maximum(sink, jnp.max(s, axis=-1, keepdims=True))
        e = jnp.exp(s - m)
        inv = 1.0 / (jnp.exp(sink - m) + jnp.sum(e, axis=-1, keepdims=True))
        p_sc[c] = (e * inv).astype(BF16)

    def values(c):
        rs = slice(c * CHUNK, (c + 1) * CHUNK)
        o = jnp.dot(p_sc[c], vc_sc[c * CHUNK:c * CHUNK + BAND], preferred_element_type=F32)
        for g in range(A_GROUP):
            o_ref[rs, g * HEAD_DIM:(g + 1) * HEAD_DIM] = o[g * CHUNK:(g + 1) * CHUNK].astype(o_ref.dtype)

    stages = (scores, softmax, values)
    for tick in range(n_chunks + len(stages) - 1):
        for si, stage in enumerate(stages):
            if 0 <= tick - si < n_chunks:
                stage(tick - si)


def _attn_a_prompt(q, k, v, table, sinks):
    b, s, _ = q.shape
    tq = min(512, s)
    prev = WINDOW_CHUNKS * CHUNK
    r = tq // prev
    k_loc = np.arange(BAND)
    q_loc = prev + np.arange(CHUNK)

    def tiles(first_key):
        valid = np.broadcast_to((k_loc >= first_key)[None, :], (CHUNK, BAND))
        t = _bias_tile(table, q_loc, k_loc, valid)
        return t.reshape(A_KV_HEADS, A_GROUP * CHUNK, BAND)

    bias, bias0, bias1 = tiles(0), tiles(prev), tiles(CHUNK)
    sink_col = jnp.repeat(sinks.astype(F32).reshape(A_KV_HEADS, A_GROUP), CHUNK, axis=1)[..., None]
    qmap = lambda bb, h, i: (bb, i, h)
    pmap = lambda bb, h, i: (bb, jnp.maximum(i * r - 1, 0), h)
    bmap = lambda bb, h, i: (h, 0, 0)
    bspec = pl.BlockSpec((None, A_GROUP * CHUNK, BAND), bmap)
    return pl.pallas_call(
        functools.partial(_attn_a_prompt_kernel, tq=tq),
        grid=(b, A_KV_HEADS, s // tq),
        in_specs=[pl.BlockSpec((None, tq, A_GROUP * HEAD_DIM), qmap),
                  pl.BlockSpec((None, prev, HEAD_DIM), pmap),
                  pl.BlockSpec((None, tq, HEAD_DIM), qmap),
                  pl.BlockSpec((None, prev, HEAD_DIM), pmap),
                  pl.BlockSpec((None, tq, HEAD_DIM), qmap),
                  bspec, bspec, bspec,
                  pl.BlockSpec((None, A_GROUP * CHUNK, 1), bmap)],
        out_specs=pl.BlockSpec((None, tq, A_GROUP * HEAD_DIM), qmap),
        out_shape=jax.ShapeDtypeStruct((b, s, D_MODEL), BF16),
        scratch_shapes=[pltpu.VMEM((prev + tq, HEAD_DIM), BF16), pltpu.VMEM((prev + tq, HEAD_DIM), BF16),
                        pltpu.VMEM((tq // CHUNK, A_GROUP * CHUNK, BAND), F32),
                        pltpu.VMEM((tq // CHUNK, A_GROUP * CHUNK, BAND), BF16)],
        compiler_params=_cparams("parallel", "parallel", "parallel"),
        name="attn_a_prompt",
    )(q, k, k, v, v, bias, bias0, bias1, sink_col)


def _attn_a_sample_kernel(q_ref, kc_ref, kn_ref, vc_ref, vn_ref, bc_ref, bn_ref, sink_ref, o_ref, *, t):
    qs = jnp.concatenate([q_ref[:, g * HEAD_DIM:(g + 1) * HEAD_DIM] for g in range(A_GROUP)], axis=0)
    o = _sink_attend(qs,
                     [kc_ref[...].astype(BF16), kn_ref[...].astype(BF16)],
                     [vc_ref[...].astype(BF16), vn_ref[...].astype(BF16)],
                     [bc_ref[...], bn_ref[...]], sink_ref[...])
    for g in range(A_GROUP):
        o_ref[:, g * HEAD_DIM:(g + 1) * HEAD_DIM] = o[g * t:(g + 1) * t].astype(o_ref.dtype)


def _attn_a_sample(q, ck, cv, kn, vn, table, sinks, past_len):
    b, t, _ = q.shape
    cache = ck.shape[1]
    q_pos = past_len + np.arange(t)
    k_pos = np.concatenate([past_len - cache + np.arange(cache), q_pos])
    qc, kc = q_pos[:, None] // CHUNK, k_pos[None, :] // CHUNK
    valid = (kc <= qc) & (kc >= qc - WINDOW_CHUNKS)
    bias = _bias_tile(table, q_pos, k_pos, valid).reshape(A_KV_HEADS, A_GROUP * t, cache + t)
    sink_col = jnp.repeat(sinks.astype(F32).reshape(A_KV_HEADS, A_GROUP), t, axis=1)[..., None]
    hmap = lambda bb, h: (bb, 0, h)
    bmap = lambda bb, h: (h, 0, 0)
    return pl.pallas_call(
        functools.partial(_attn_a_sample_kernel, t=t),
        grid=(b, A_KV_HEADS),
        in_specs=[pl.BlockSpec((None, t, A_GROUP * HEAD_DIM), hmap),
                  pl.BlockSpec((None, cache, HEAD_DIM), hmap),
                  pl.BlockSpec((None, t, HEAD_DIM), hmap),
                  pl.BlockSpec((None, cache, HEAD_DIM), hmap),
                  pl.BlockSpec((None, t, HEAD_DIM), hmap),
                  pl.BlockSpec((None, A_GROUP * t, cache), bmap),
                  pl.BlockSpec((None, A_GROUP * t, t), bmap),
                  pl.BlockSpec((None, A_GROUP * t, 1), bmap)],
        out_specs=pl.BlockSpec((None, t, A_GROUP * HEAD_DIM), hmap),
        out_shape=jax.ShapeDtypeStruct((b, t, D_MODEL), BF16),
        compiler_params=_cparams("parallel", "parallel"),
        name="attn_a_sample",
    )(q, ck, kn, cv, vn, bias[..., :cache], bias[..., cache:], sink_col)


def _diff_lambda(lam_ref, lam_init):
    lp = lam_ref[...]
    a = jnp.sum(lp[0:1] * lp[1:2], axis=-1, keepdims=True)
    c = jnp.sum(lp[2:3] * lp[3:4], axis=-1, keepdims=True)
    return jnp.exp(a) - jnp.exp(c) + lam_init


def _finish_b(o0, o1, lam, gsub, lam_init):
    o = o0 - lam * o1
    o = o * lax.rsqrt(jnp.mean(o * o, axis=-1, keepdims=True) + EPS) * gsub
    return o * (1.0 - lam_init)


def _attn_b_prompt_kernel(q_ref, k_ref, v_ref, bias_ref, lam_ref, gsub_ref, o_ref,
                          s0_sc, s1_sc, p0_sc, p1_sc, al_sc, m_sc, l_sc, acc_sc, *, t, lam_init):
    i = pl.program_id(2)
    s_scs = (s0_sc, s1_sc)
    p_scs = (p0_sc, p1_sc)
    m_sc[...] = jnp.full(m_sc.shape, MASKED, F32)
    l_sc[...] = jnp.zeros(l_sc.shape, F32)
    acc_sc[...] = jnp.zeros(acc_sc.shape, F32)

    def keys(j):
        return pl.ds(pl.multiple_of(j * t, t), t)

    def scores(j, mp):
        sl = slice(mp * HEAD_DIM, (mp + 1) * HEAD_DIM)
        s_scs[mp][...] = lax.dot_general(q_ref[:, sl], k_ref[keys(j), sl], NT_DIMS, preferred_element_type=F32)

    def softmax(j, mp):
        kind = jnp.minimum(i - j, 2)
        s = s_scs[mp][...] + bias_ref[mp, kind]
        m_old = m_sc[mp]
        m_new = jnp.maximum(m_old, jnp.max(s, axis=-1, keepdims=True))
        alpha = jnp.exp2(m_old - m_new)
        p = jnp.exp2(s - m_new)
        l_sc[mp] = alpha * l_sc[mp] + jnp.sum(p, axis=-1, keepdims=True)
        m_sc[mp] = m_new
        al_sc[mp] = alpha
        p_scs[mp][...] = p.astype(BF16)

    def weigh(j, mp):
        acc_sc[mp] = al_sc[mp] * acc_sc[mp] + jnp.dot(p_scs[mp][...], v_ref[keys(j), :],
                                                      preferred_element_type=F32)

    scores(0, 0)
    scores(0, 1)
    softmax(0, 0)

    def body(j, carry):
        scores(j, 0)
        softmax(j - 1, 1)
        weigh(j - 1, 0)
        scores(j, 1)
        softmax(j, 0)
        weigh(j - 1, 1)
        return carry

    lax.fori_loop(1, i + 1, body, 0)
    softmax(i, 1)
    weigh(i, 0)
    weigh(i, 1)
    lam = _diff_lambda(lam_ref, lam_init)
    o = _finish_b(acc_sc[0] / l_sc[0], acc_sc[1] / l_sc[1], lam, gsub_ref[...], lam_init)
    o_ref[...] = o.astype(o_ref.dtype)


def _toeplitz(f, t):
    h = f.shape[0]
    g = jnp.pad(f, ((0, 0), (0, 1)))
    flat = jnp.tile(g, (1, t))[:, :t * (2 * t - 1)]
    return flat.reshape(h, t, 2 * t - 1)[:, :, t - 1:]


def _attn_b_prompt(q, k, v, table, lam_rows, gsub, lam_init):
    b, s, _ = q.shape
    t = min(512, s)
    loc = np.arange(t)
    chunk_ok = jnp.asarray((loc[None, :] // CHUNK) <= (loc[:, None] // CHUNK))
    tab = table.astype(F32) * LOG2E
    rel = np.arange(-(t - 1), t)
    diag = _toeplitz(tab[jnp.asarray(_t5_bucket_np(rel))].T, t)
    diag = jnp.where(chunk_ok[None], diag, MASKED)
    sub = _toeplitz(tab[jnp.asarray(_t5_bucket_np(rel - t))].T, t)
    far_rel = -(t + 1 + np.arange(max(s - t, 1)))
    far_bucket = _t5_bucket_np(far_rel)
    assert np.all(far_bucket == far_bucket[0])
    far = jnp.broadcast_to(tab[int(far_bucket[0])][:, None, None], (2 * B_HEADS, t, t))
    bias = jnp.stack([diag, sub, far], axis=1)
    qmap = lambda bb, h, i: (bb, i, h)
    kmap = lambda bb, h, i: (bb, 0, h)
    return pl.pallas_call(
        functools.partial(_attn_b_prompt_kernel, t=t, lam_init=lam_init),
        grid=(b, B_HEADS, s // t),
        in_specs=[pl.BlockSpec((None, t, B_VDIM), qmap),
                  pl.BlockSpec((None, s, B_VDIM), kmap),
                  pl.BlockSpec((None, s, B_VDIM), kmap),
                  pl.BlockSpec((2, 3, t, t), lambda bb, h, i: (h, 0, 0, 0)),
                  pl.BlockSpec((4, HEAD_DIM), lambda bb, h, i: (0, 0)),
                  pl.BlockSpec((1, B_VDIM), lambda bb, h, i: (0, 0))],
        out_specs=pl.BlockSpec((None, t, B_VDIM), qmap),
        out_shape=jax.ShapeDtypeStruct((b, s, D_MODEL), BF16),
        scratch_shapes=[pltpu.VMEM((t, t), F32), pltpu.VMEM((t, t), F32),
                        pltpu.VMEM((t, t), BF16), pltpu.VMEM((t, t), BF16),
                        pltpu.VMEM((2, t, 1), F32), pltpu.VMEM((2, t, 1), F32), pltpu.VMEM((2, t, 1), F32),
                        pltpu.VMEM((2, t, B_VDIM), F32)],
        compiler_params=_cparams("parallel", "parallel", "parallel"),
        name="attn_b_prompt",
    )(q, k, v, bias, lam_rows, gsub.reshape(1, B_VDIM).astype(F32))


def _attn_b_sample_kernel(q_ref, kc_ref, kn_ref, vc_ref, vn_ref, bc_ref, bn_ref, lam_ref, gsub_ref, o_ref,
                          *, lam_init):
    kc = kc_ref[...].astype(BF16)
    kn = kn_ref[...].astype(BF16)
    vc = vc_ref[...].astype(BF16)
    vn = vn_ref[...].astype(BF16)
    outs = []
    for mp in range(2):
        sl = slice(mp * HEAD_DIM, (mp + 1) * HEAD_DIM)
        qm = q_ref[:, sl]
        sc = lax.dot_general(qm, kc[:, sl], NT_DIMS, preferred_element_type=F32) + bc_ref[mp]
        sn = lax.dot_general(qm, kn[:, sl], NT_DIMS, preferred_element_type=F32) + bn_ref[mp]
        m = jnp.maximum(jnp.max(sc, axis=-1, keepdims=True), jnp.max(sn, axis=-1, keepdims=True))
        pc = jnp.exp(sc - m)
        pn = jnp.exp(sn - m)
        inv = 1.0 / (jnp.sum(pc, axis=-1, keepdims=True) + jnp.sum(pn, axis=-1, keepdims=True))
        outs.append(jnp.dot((pc * inv).astype(BF16), vc, preferred_element_type=F32)
                    + jnp.dot((pn * inv).astype(BF16), vn, preferred_element_type=F32))
    lam = _diff_lambda(lam_ref, lam_init)
    o_ref[...] = _finish_b(outs[0], outs[1], lam, gsub_ref[...], lam_init).astype(o_ref.dtype)


def _attn_b_sample(q, ck, cv, kn, vn, table, lam_rows, gsub, lam_init, past_len):
    b, t, _ = q.shape
    past = ck.shape[1]
    q_pos = past_len + np.arange(t)
    k_pos = np.arange(past_len + t)
    valid = (k_pos[None, :] // CHUNK) <= (q_pos[:, None] // CHUNK)
    bias = _bias_tile(table, q_pos, k_pos, valid).reshape(B_HEADS, 2, t, past + t)
    hmap = lambda bb, h: (bb, 0, h)
    bmap = lambda bb, h: (h, 0, 0, 0)
    return pl.pallas_call(
        functools.partial(_attn_b_sample_kernel, lam_init=lam_init),
        grid=(b, B_HEADS),
        in_specs=[pl.BlockSpec((None, t, B_VDIM), hmap),
                  pl.BlockSpec((None, past, B_VDIM), hmap),
                  pl.BlockSpec((None, t, B_VDIM), hmap),
                  pl.BlockSpec((None, past, B_VDIM), hmap),
                  pl.BlockSpec((None, t, B_VDIM), hmap),
                  pl.BlockSpec((None, 2, t, past), bmap),
                  pl.BlockSpec((None, 2, t, t), bmap),
                  pl.BlockSpec((4, HEAD_DIM), lambda bb, h: (0, 0)),
                  pl.BlockSpec((1, B_VDIM), lambda bb, h: (0, 0))],
        out_specs=pl.BlockSpec((None, t, B_VDIM), hmap),
        out_shape=jax.ShapeDtypeStruct((b, t, D_MODEL), BF16),
        compiler_params=_cparams("parallel", "parallel"),
        name="attn_b_sample",
    )(q, ck, kn, cv, vn, bias[..., :past], bias[..., past:], lam_rows, gsub.reshape(1, B_VDIM).astype(F32))


def _neg_abs(x):
    bits = lax.bitcast_convert_type(x, jnp.uint32) | jnp.uint32(0x80000000)
    return lax.bitcast_convert_type(bits, F32)


def _minus_later_ones(n):
    idx = np.arange(n)
    return jnp.asarray(-(idx[:, None] > idx[None, :]).astype(np.float32), dtype=BF16)


def _sb_block(q, kb, vb, u, run, mask):
    z = lax.dot_general(q, kb, NT_DIMS, preferred_element_type=F32)
    sp = jnp.maximum(z, 0.0) + jnp.log(1.0 + jnp.exp(-jnp.abs(z)))
    if mask is not None:
        sp = jnp.where(mask, sp, 0.0)
    after = jnp.dot(sp.astype(BF16), u, preferred_element_type=F32) + run
    a = jnp.exp((z - sp) + after)
    if mask is not None:
        a = jnp.where(mask, a, 0.0)
    o = jnp.dot(a.astype(BF16), vb, preferred_element_type=F32)
    return o, run - jnp.sum(sp, axis=-1, keepdims=True)


C_HEADS_PER_STEP = 4


def _attn_c_prompt_kernel(q_ref, k_ref, v_ref, u_ref, o_ref, z_sc, spb_sc, aft_sc, a_sc, off_sc, run_sc, acc_sc,
                          *, t):
    i = pl.program_id(2)
    hp = C_HEADS_PER_STEP
    heads = [slice(h * HEAD_DIM, (h + 1) * HEAD_DIM) for h in range(hp)]
    run_sc[...] = jnp.zeros(run_sc.shape, F32)
    acc_sc[...] = jnp.zeros(acc_sc.shape, F32)

    def block(start, masked):
        keys = pl.ds(start, t)
        if masked:
            visible = lax.broadcasted_iota(jnp.int32, (t, t), 1) < lax.broadcasted_iota(jnp.int32, (t, t), 0)

        def scores(h):
            z_sc[h] = lax.dot_general(q_ref[:, heads[h]], k_ref[keys, heads[h]], NT_DIMS,
                                      preferred_element_type=F32)

        def softplus(h):
            z = z_sc[h]
            sp = jnp.maximum(z, 0.0) + jnp.log(1.0 + jnp.exp(_neg_abs(z)))
            if masked:
                sp = jnp.where(visible, sp, 0.0)
            z_sc[h] = z - sp
            spb_sc[h] = sp.astype(BF16)
            run = run_sc[h]
            off_sc[h] = jnp.exp(run)
            run_sc[h] = run - jnp.sum(sp, axis=-1, keepdims=True)

        def later_sums(h):
            aft_sc[h] = jnp.dot(spb_sc[h], u_ref[...], preferred_element_type=F32)

        def weights(h):
            a = jnp.exp(z_sc[h] + aft_sc[h])
            if masked:
                a = jnp.where(visible, a, 0.0)
            a_sc[h] = a.astype(BF16)

        def values(h):
            acc_sc[h] += off_sc[h] * jnp.dot(a_sc[h], v_ref[keys, heads[h]], preferred_element_type=F32)

        stages = (scores, softplus, later_sums, weights, values)
        for tick in range(hp + len(stages) - 1):
            for si, stage in enumerate(stages):
                if 0 <= tick - si < hp:
                    stage(tick - si)

    block(pl.multiple_of(i * t, t), True)

    def body(n, carry):
        block(pl.multiple_of((i - 1 - n) * t, t), False)
        return carry

    lax.fori_loop(0, i, body, 0)
    for h in range(hp):
        o_ref[:, heads[h]] = acc_sc[h].astype(o_ref.dtype)


def _attn_c_prompt(q, k, v):
    b, s, _ = q.shape
    t = min(256, s)
    hp = C_HEADS_PER_STEP
    qmap = lambda bb, h, i: (bb, i, h)
    kmap = lambda bb, h, i: (bb, 0, h)
    return pl.pallas_call(
        functools.partial(_attn_c_prompt_kernel, t=t),
        grid=(b, C_HEADS // hp, s // t),
        in_specs=[pl.BlockSpec((None, t, hp * HEAD_DIM), qmap),
                  pl.BlockSpec((None, s, hp * HEAD_DIM), kmap),
                  pl.BlockSpec((None, s, hp * HEAD_DIM), kmap),
                  pl.BlockSpec((t, t), lambda bb, h, i: (0, 0))],
        out_specs=pl.BlockSpec((None, t, hp * HEAD_DIM), qmap),
        out_shape=jax.ShapeDtypeStruct((b, s, D_MODEL), BF16),
        scratch_shapes=[pltpu.VMEM((hp, t, t), F32), pltpu.VMEM((hp, t, t), BF16), pltpu.VMEM((hp, t, t), F32),
                        pltpu.VMEM((hp, t, t), BF16), pltpu.VMEM((hp, t, 1), F32),
                        pltpu.VMEM((hp, t, 1), F32), pltpu.VMEM((hp, t, HEAD_DIM), F32)],
        compiler_params=_cparams("parallel", "parallel", "parallel"),
        name="attn_c_prompt",
    )(q, k, v, _minus_later_ones(t))


def _attn_c_sample_kernel(q_ref, kc_ref, kn_ref, vc_ref, vn_ref, u_ref, o_ref, *, t, past, tk):
    q = q_ref[...]
    u = u_ref[...]
    row = lax.broadcasted_iota(jnp.int32, (t, t), 0)
    col = lax.broadcasted_iota(jnp.int32, (t, t), 1)
    acc, run = _sb_block(q, kn_ref[...].astype(BF16), vn_ref[...].astype(BF16), u[:t, :t],
                         jnp.zeros((t, 1), F32), col < row)
    for j in reversed(range(past // tk)):
        rs = slice(j * tk, (j + 1) * tk)
        o, run = _sb_block(q, kc_ref[rs, :].astype(BF16), vc_ref[rs, :].astype(BF16), u, run, None)
        acc = acc + o
    o_ref[...] = acc.astype(o_ref.dtype)


def _attn_c_sample(q, ck, cv, kn, vn):
    b, t, _ = q.shape
    past = ck.shape[1]
    tk = min(256, past)
    hmap = lambda bb, h: (bb, 0, h)
    return pl.pallas_call(
        functools.partial(_attn_c_sample_kernel, t=t, past=past, tk=tk),
        grid=(b, C_HEADS),
        in_specs=[pl.BlockSpec((None, t, HEAD_DIM), hmap),
                  pl.BlockSpec((None, past, HEAD_DIM), hmap),
                  pl.BlockSpec((None, t, HEAD_DIM), hmap),
                  pl.BlockSpec((None, past, HEAD_DIM), hmap),
                  pl.BlockSpec((None, t, HEAD_DIM), hmap),
                  pl.BlockSpec((tk, tk), lambda bb, h: (0, 0))],
        out_specs=pl.BlockSpec((None, t, HEAD_DIM), hmap),
        out_shape=jax.ShapeDtypeStruct((b, t, D_MODEL), BF16),
        compiler_params=_cparams("parallel", "parallel"),
        name="attn_c_sample",
    )(q, ck, kn, cv, vn, _minus_later_ones(tk))


ROUTE_LANES = LANES
EXPERT_LANE0 = N_GROUPS
ROW_LANES = LANES
ROW_TILE = D_MODEL // ROW_LANES


def _store_row_tiled(ref, val):
    rows = val.shape[0]
    for s in range(ROW_TILE):
        ref[pl.ds(s, rows, stride=ROW_TILE), :] = val[:, s * ROW_LANES:(s + 1) * ROW_LANES]


def _load_row_tiled(ref, rows, s, base=0):
    return ref[pl.ds(base * ROW_TILE + s, rows, stride=ROW_TILE), :]


def _router_kernel(x_ref, g_ref, sc_ref, sh_ref, w_ref, b_ref, h_ref, r_ref):
    h = _norm_mod_f32(x_ref[...], g_ref[...], sc_ref[...], sh_ref[...])
    _store_row_tiled(h_ref, h)
    h_hi = h.astype(BF16)
    h_lo = (h - h_hi.astype(F32)).astype(BF16)
    w = w_ref[...]
    both = jnp.dot(h_hi, w, preferred_element_type=F32)
    logits = (both[:, :ROUTE_LANES] + both[:, ROUTE_LANES:]
              + jnp.dot(h_lo, w[:, :ROUTE_LANES], preferred_element_type=F32) + b_ref[...])
    lane = lax.broadcasted_iota(jnp.int32, logits.shape, 1)
    lanef = lane.astype(F32)

    def first_lane_of_max(v, vmax):
        return jnp.min(jnp.where(v == vmax, lanef, float(ROUTE_LANES)), axis=-1, keepdims=True)

    lg = jnp.where(lane < N_GROUPS, logits, MASKED)
    mg = jnp.max(lg, axis=-1, keepdims=True)
    gate = 1.0 / jnp.sum(jnp.exp(lg - mg), axis=-1, keepdims=True)
    gi = first_lane_of_max(lg, mg)
    lo = EXPERT_LANE0 + EXPERTS_PER_GROUP * gi
    le = jnp.where((lanef >= lo) & (lanef < lo + EXPERTS_PER_GROUP), logits, MASKED)
    v1 = jnp.max(le, axis=-1, keepdims=True)
    i1 = first_lane_of_max(le, v1)
    le2 = jnp.where(lanef == i1, MASKED, le)
    v2 = jnp.max(le2, axis=-1, keepdims=True)
    i2 = first_lane_of_max(le2, v2)
    e21 = jnp.exp(v2 - v1)
    w1 = gate / (1.0 + e21)
    w2 = w1 * e21
    r_ref[...] = jnp.where(lane == 0, i1 - EXPERT_LANE0,
                           jnp.where(lane == 1, i2 - EXPERT_LANE0,
                                     jnp.where(lane == 2, w1, jnp.where(lane == 3, w2, 0.0))))


def _router(x, gain, sc, sh, w_cat, b_row):
    b, s, d = x.shape
    tm = min(512, s)
    rows = lambda bb, i: (bb, i, 0)
    one = lambda bb, i: (bb, 0, 0)
    return pl.pallas_call(
        _router_kernel,
        grid=(b, s // tm),
        in_specs=[pl.BlockSpec((None, tm, d), rows),
                  pl.BlockSpec((1, d), lambda bb, i: (0, 0)),
                  _mod_spec(sc, tm, d, rows, one),
                  _mod_spec(sh, tm, d, rows, one),
                  pl.BlockSpec((d, 2 * ROUTE_LANES), lambda bb, i: (0, 0)),
                  pl.BlockSpec((1, ROUTE_LANES), lambda bb, i: (0, 0))],
        out_specs=[pl.BlockSpec((None, tm * ROW_TILE, ROW_LANES), rows),
                   pl.BlockSpec((None, tm, ROUTE_LANES), rows)],
        out_shape=[jax.ShapeDtypeStruct((b, s * ROW_TILE, ROW_LANES), F32),
                   jax.ShapeDtypeStruct((b, s, ROUTE_LANES), F32)],
        compiler_params=_cparams("parallel", "parallel"),
        name="moe_router",
    )(x, gain.reshape(1, d), sc, sh, w_cat, b_row)


def _router_weights(w_group, b_group, w_router, b_router):
    w = jnp.concatenate([w_group, jnp.moveaxis(w_router, 0, 1).reshape(D_MODEL, N_EXPERTS)], axis=1)
    w = jnp.pad(w.astype(F32), ((0, 0), (0, ROUTE_LANES - w.shape[1])))
    hi = w.astype(BF16)
    lo = (w - hi.astype(F32)).astype(BF16)
    bias = jnp.concatenate([b_group, b_router.reshape(N_EXPERTS)]).astype(F32)
    bias = jnp.pad(bias, (0, ROUTE_LANES - bias.shape[0])).reshape(1, ROUTE_LANES)
    return jnp.concatenate([hi, lo], axis=1), bias


def _row_gather(src_hbm, idx_ref, n_rows, dst, sem, inline=False):
    def start(r):
        src = pl.multiple_of(idx_ref[0, r] * ROW_TILE, ROW_TILE)
        static = isinstance(r, int)
        dst_rows = r * ROW_TILE if static else pl.multiple_of(r * ROW_TILE, ROW_TILE)
        pltpu.make_async_copy(src_hbm.at[pl.ds(src, ROW_TILE)], dst.at[pl.ds(dst_rows, ROW_TILE)],
                              sem).start(priority=r % 2 if static else 0)

    if inline:
        for r in range(n_rows):
            start(r)
    else:
        def body(r, carry):
            start(r)
            return carry
        lax.fori_loop(0, n_rows, body, 0, unroll=8)


def _row_gather_wait(src_hbm, n_rows, dst, sem):
    pltpu.make_async_copy(src_hbm.at[pl.ds(0, n_rows * ROW_TILE)], dst, sem).wait()


def _expert_kernel(te_ref, idx_ref, idx_next_ref, h_hbm, wgu_ref, wd_ref, y_ref, buf0, buf1, sem, *, tm):
    i = pl.program_id(0)
    n = pl.num_programs(0)

    @pl.when(i == 0)
    def _():
        _row_gather(h_hbm, idx_ref, tm, buf0, sem.at[0])

    def tile(cur, cur_sem, nxt, nxt_sem):
        _row_gather_wait(h_hbm, tm, cur, cur_sem)
        _row_gather(h_hbm, idx_next_ref, tm, nxt, nxt_sem, inline=True)
        x = jnp.concatenate([_load_row_tiled(cur, tm, s).astype(BF16) for s in range(ROW_TILE)], axis=1)
        gu = jnp.dot(x, wgu_ref[...], preferred_element_type=F32)
        gate, up = gu[:, :D_EXPERT], gu[:, D_EXPERT:]
        hid = (gate / (1.0 + jnp.exp(-gate)) * up).astype(BF16)
        _store_row_tiled(y_ref, jnp.dot(hid, wd_ref[...], preferred_element_type=F32))

        @pl.when(i == n - 1)
        def _():
            _row_gather_wait(h_hbm, tm, nxt, nxt_sem)

    @pl.when(i % 2 == 0)
    def _():
        tile(buf0, sem.at[0], buf1, sem.at[1])

    @pl.when(i % 2 == 1)
    def _():
        tile(buf1, sem.at[1], buf0, sem.at[0])


def _experts(h_rt, tile_expert, src_rows, w_gu, w_down, tm):
    n_tiles = tile_expert.shape[0]
    d = D_MODEL
    grid_spec = pltpu.PrefetchScalarGridSpec(
        num_scalar_prefetch=1,
        grid=(n_tiles,),
        in_specs=[pl.BlockSpec((None, 1, tm), lambda i, te: (i, 0, 0), memory_space=pltpu.SMEM),
                  pl.BlockSpec((None, 1, tm), lambda i, te: (jnp.minimum(i + 1, n_tiles - 1), 0, 0),
                               memory_space=pltpu.SMEM),
                  pl.BlockSpec(memory_space=pl.ANY),
                  pl.BlockSpec((None, d, 2 * D_EXPERT), lambda i, te: (te[i], 0, 0)),
                  pl.BlockSpec((None, D_EXPERT, d), lambda i, te: (te[i], 0, 0))],
        out_specs=pl.BlockSpec((tm * ROW_TILE, ROW_LANES), lambda i, te: (i, 0)),
        scratch_shapes=[pltpu.VMEM((tm * ROW_TILE, ROW_LANES), F32), pltpu.VMEM((tm * ROW_TILE, ROW_LANES), F32),
                        pltpu.SemaphoreType.DMA((2,))],
    )
    idx3 = src_rows.reshape(n_tiles, 1, tm)
    return pl.pallas_call(
        functools.partial(_expert_kernel, tm=tm),
        grid_spec=grid_spec,
        out_shape=jax.ShapeDtypeStruct((n_tiles * tm * ROW_TILE, ROW_LANES), F32),
        compiler_params=_cparams("arbitrary"),
        name="moe_experts",
    )(tile_expert, idx3, idx3, h_rt, w_gu, w_down)


def _combine_kernel(pos_ref, pos_next_ref, y_hbm, x_ref, g_ref, r_ref, o_ref, buf0, buf1, sem, *, tc):
    i = pl.program_id(0) * pl.num_programs(1) + pl.program_id(1)
    n = pl.num_programs(0) * pl.num_programs(1)

    @pl.when(i == 0)
    def _():
        _row_gather(y_hbm, pos_ref, 2 * tc, buf0, sem.at[0])

    def tile(cur, cur_sem, nxt, nxt_sem):
        _row_gather_wait(y_hbm, 2 * tc, cur, cur_sem)
        _row_gather(y_hbm, pos_next_ref, 2 * tc, nxt, nxt_sem, inline=True)
        w1 = r_ref[:, 2:3]
        w2 = r_ref[:, 3:4]
        for s in range(ROW_TILE):
            sl = slice(s * ROW_LANES, (s + 1) * ROW_LANES)
            y = w1 * _load_row_tiled(cur, tc, s) + w2 * _load_row_tiled(cur, tc, s, base=tc)
            o_ref[:, sl] = x_ref[:, sl] + g_ref[:, sl] * y

        @pl.when(i == n - 1)
        def _():
            _row_gather_wait(y_hbm, 2 * tc, nxt, nxt_sem)

    @pl.when(i % 2 == 0)
    def _():
        tile(buf0, sem.at[0], buf1, sem.at[1])

    @pl.when(i % 2 == 1)
    def _():
        tile(buf1, sem.at[1], buf0, sem.at[0])


def _combine(x, y_rt, pos, gate, route, tc):
    b, s, d = x.shape
    nt = s // tc
    n = b * nt
    pos3 = pos.reshape(n, tc, 2).transpose(0, 2, 1).reshape(n, 1, 2 * tc)
    rows = lambda bb, i: (bb, i, 0)
    one = lambda bb, i: (bb, 0, 0)
    return pl.pallas_call(
        functools.partial(_combine_kernel, tc=tc),
        grid=(b, nt),
        in_specs=[pl.BlockSpec((None, 1, 2 * tc), lambda bb, i: (bb * nt + i, 0, 0), memory_space=pltpu.SMEM),
                  pl.BlockSpec((None, 1, 2 * tc), lambda bb, i: (jnp.minimum(bb * nt + i + 1, n - 1), 0, 0),
                               memory_space=pltpu.SMEM),
                  pl.BlockSpec(memory_space=pl.ANY),
                  pl.BlockSpec((None, tc, d), rows),
                  _mod_spec(gate, tc, d, rows, one),
                  pl.BlockSpec((None, tc, ROUTE_LANES), rows)],
        out_specs=pl.BlockSpec((None, tc, d), rows),
        out_shape=jax.ShapeDtypeStruct((b, s, d), F32),
        scratch_shapes=[pltpu.VMEM((2 * tc * ROW_TILE, ROW_LANES), F32),
                        pltpu.VMEM((2 * tc * ROW_TILE, ROW_LANES), F32), pltpu.SemaphoreType.DMA((2,))],
        compiler_params=_cparams("arbitrary", "arbitrary"),
        name="moe_combine",
    )(pos3, pos3, y_rt, x, gate, route)


def _moe(x, gain, sc, sh, gate, wts):
    w_cat, b_row, w_gu, w_down = wts
    b, s, d = x.shape
    tokens = b * s
    h, route = _router(x, gain, sc, sh, w_cat, b_row)
    ids = route.reshape(tokens, ROUTE_LANES)[:, 0:2].astype(jnp.int32).reshape(-1)
    tm = 256 if tokens >= 4096 else 128
    n_assign = 2 * tokens
    n_tiles = (n_assign + N_EXPERTS * (tm - 1)) // tm + 1
    onehot = (ids[:, None] == jnp.arange(N_EXPERTS)[None, :]).astype(jnp.int32)
    csum = jnp.cumsum(onehot, axis=0)
    counts = csum[-1]
    padded = ((counts + tm - 1) // tm) * tm
    ends = jnp.cumsum(padded)
    pos = jnp.sum(onehot * (csum - 1 + (ends - padded)[None, :]), axis=1)
    tile_expert = jnp.minimum(jnp.searchsorted(ends // tm, jnp.arange(n_tiles), side='right'),
                              N_EXPERTS - 1).astype(jnp.int32)
    src_rows = jnp.zeros((n_tiles * tm,), jnp.int32).at[pos].set(jnp.arange(n_assign, dtype=jnp.int32) // 2)
    y = _experts(h.reshape(tokens * ROW_TILE, ROW_LANES), tile_expert, src_rows, w_gu, w_down, tm)
    return _combine(x, y, pos.reshape(tokens, 2), gate, route, min(256, s))


def _split_mod(mod):
    return [mod[..., j * D_MODEL:(j + 1) * D_MODEL] for j in range(6)]


def kernel(x_prompt, x_sample, c_prompt, c_sample, cache_a_k, cache_a_v, cache_b_k, cache_b_v, cache_c_k, cache_c_v, rel_bias_table, norm_mix, norm_ffn, w_ada, b_ada, w_in_a, q_gain_a, k_gain_a, sinks_a, w_out_a, w_in_b, q_gain_b, k_gain_b, lam_q1, lam_k1, lam_q2, lam_k2, sub_gain_b, w_out_b, w_in_c, w_out_c, w_group, b_group, w_router, b_router, w_gate, w_up, w_down):
    nb, seq, d = x_prompt.shape
    db, dt, _ = x_sample.shape
    past_len = cache_b_k.shape[2]
    a_cache = cache_a_k.shape[2]
    ns = db * dt

    n_c = nb + db
    c_rows = -(-n_c // 16) * 16
    c_all = jnp.pad(jnp.concatenate([c_prompt, c_sample], axis=0).astype(F32), ((0, c_rows - n_c), (0, 0)))
    mods = _ada_mod(c_all, w_ada, b_ada)

    xp = x_prompt
    xs = x_sample.reshape(1, ns, d)
    st_p, st_s = [], []
    for l in range(DEPTH):
        i, kind = l // N_MIXERS, l % N_MIXERS
        mp = _split_mod(mods[l, :nb][:, None, :])
        ms = _split_mod(jnp.repeat(mods[l, nb:n_c], dt, axis=0)[None])
        hp = _norm_mod(xp, norm_mix[l], mp[1], mp[0])
        hs = _norm_mod(xs, norm_mix[l], ms[1], ms[0])

        if kind == 0:
            nq, nk = A_HEADS * HEAD_DIM, A_KV_HEADS * HEAD_DIM
            wq = w_in_a[i][:, :nq].astype(BF16)
            wk = w_in_a[i][:, nq:nq + nk].astype(BF16)
            wv = w_in_a[i][:, nq + nk:].astype(BF16)
            wo = w_out_a[i].astype(BF16)
            q = _proj(hp, wq, q_gain_a[i], QK_SCALE)
            k32, kbf = _proj(hp, wk, k_gain_a[i], out32=True)
            v32, vbf = _proj(hp, wv, out32=True)
            o = _attn_a_prompt(q, kbf, vbf, rel_bias_table, sinks_a[i])
            xp = _out_res(o, wo, xp, mp[2])
            st_p.append((k32[:, -a_cache:].reshape(nb, a_cache, A_KV_HEADS, HEAD_DIM),
                         v32[:, -a_cache:].reshape(nb, a_cache, A_KV_HEADS, HEAD_DIM)))
            q = _proj(hs, wq, q_gain_a[i], QK_SCALE).reshape(db, dt, nq)
            kn = _proj(hs, wk, k_gain_a[i], out32=True, outbf=False).reshape(db, dt, nk)
            vn = _proj(hs, wv, out32=True, outbf=False).reshape(db, dt, nk)
            ck = cache_a_k[i].reshape(db, a_cache, nk)
            cv = cache_a_v[i].reshape(db, a_cache, nk)
            o = _attn_a_sample(q, ck, cv, kn, vn, rel_bias_table, sinks_a[i], past_len)
            xs = _out_res(o.reshape(1, ns, d), wo, xs, ms[2])
            kk = jnp.concatenate([ck, kn], axis=1)[:, -a_cache:]
            vv = jnp.concatenate([cv, vn], axis=1)[:, -a_cache:]
            st_s.append((kk.reshape(db, a_cache, A_KV_HEADS, HEAD_DIM),
                         vv.reshape(db, a_cache, A_KV_HEADS, HEAD_DIM)))
        elif kind == 1:
            nq = B_HEADS * 2 * HEAD_DIM
            lam_init = _lambda_init(l)
            wq = w_in_b[i][:, :nq].astype(BF16)
            wk = w_in_b[i][:, nq:2 * nq].astype(BF16)
            wv = w_in_b[i][:, 2 * nq:].astype(BF16)
            wo = w_out_b[i].astype(BF16)
            lam_rows = jnp.stack([lam_q1[i], lam_k1[i], lam_q2[i], lam_k2[i]]).astype(F32)
            q = _proj(hp, wq, q_gain_b[i], QK_SCALE * LOG2E)
            k32, kbf = _proj(hp, wk, k_gain_b[i], out32=True)
            v32, vbf = _proj(hp, wv, out32=True)
            o = _attn_b_prompt(q, kbf, vbf, rel_bias_table, lam_rows, sub_gain_b[i], lam_init)
            xp = _out_res(o, wo, xp, mp[2])
            st_p.append((k32.reshape(nb, seq, B_HEADS, 2, HEAD_DIM), v32.reshape(nb, seq, B_HEADS, B_VDIM)))
            q = _proj(hs, wq, q_gain_b[i], QK_SCALE).reshape(db, dt, nq)
            kn = _proj(hs, wk, k_gain_b[i], out32=True, outbf=False).reshape(db, dt, nq)
            vn = _proj(hs, wv, out32=True, outbf=False).reshape(db, dt, nq)
            o = _attn_b_sample(q, cache_b_k[i].reshape(db, past_len, nq), cache_b_v[i].reshape(db, past_len, nq),
                               kn, vn, rel_bias_table, lam_rows, sub_gain_b[i], lam_init, past_len)
            xs = _out_res(o.reshape(1, ns, d), wo, xs, ms[2])
            st_s.append((kn.reshape(db, dt, B_HEADS, 2, HEAD_DIM), vn.reshape(db, dt, B_HEADS, B_VDIM)))
        else:
            wq = w_in_c[i][:, :d].astype(BF16)
            wk = w_in_c[i][:, d:2 * d].astype(BF16)
            wv = w_in_c[i][:, 2 * d:].astype(BF16)
            wo = w_out_c[i].astype(BF16)
            q = _proj(hp, wq, None, QK_SCALE)
            k32, kbf = _proj(hp, wk, out32=True)
            v32, vbf = _proj(hp, wv, out32=True)
            o = _attn_c_prompt(q, kbf, vbf)
            xp = _out_res(o, wo, xp, mp[2])
            st_p.append((k32.reshape(nb, seq, C_HEADS, HEAD_DIM), v32.reshape(nb, seq, C_HEADS, HEAD_DIM)))
            q = _proj(hs, wq, None, QK_SCALE).reshape(db, dt, d)
            kn = _proj(hs, wk, out32=True, outbf=False).reshape(db, dt, d)
            vn = _proj(hs, wv, out32=True, outbf=False).reshape(db, dt, d)
            o = _attn_c_sample(q, cache_c_k[i].reshape(db, past_len, d), cache_c_v[i].reshape(db, past_len, d),
                               kn, vn)
            xs = _out_res(o.reshape(1, ns, d), wo, xs, ms[2])
            st_s.append((kn.reshape(db, dt, C_HEADS, HEAD_DIM), vn.reshape(db, dt, C_HEADS, HEAD_DIM)))

        w_cat, b_row = _router_weights(w_group[l], b_group[l], w_router[l], b_router[l])
        moe_w = (w_cat, b_row,
                 jnp.concatenate([w_gate[l], w_up[l]], axis=-1).astype(BF16),
                 w_down[l].astype(BF16))
        xp = _moe(xp, norm_ffn[l], mp[4], mp[3], mp[5], moe_w)
        xs = _moe(xs, norm_ffn[l], ms[4], ms[3], ms[5], moe_w)

    def collect(states, kind, j):
        parts = [states[l][j] for l in range(DEPTH) if l % N_MIXERS == kind]
        return parts[0][None] if len(parts) == 1 else jnp.stack(parts)

    return (xp, xs.reshape(db, dt, d),
            collect(st_p, 0, 0), collect(st_p, 0, 1),
            collect(st_p, 1, 0), collect(st_p, 1, 1),
            collect(st_p, 2, 0), collect(st_p, 2, 1),
            collect(st_s, 0, 0), collect(st_s, 0, 1),
            collect(st_s, 1, 0), collect(st_s, 1, 1),
            collect(st_s, 2, 0), collect(st_s, 2, 1))
```

```python
import functools
import math

import numpy as np
import jax
import jax.numpy as jnp
from jax import lax
from jax.experimental import pallas as pl
from jax.experimental.pallas import tpu as pltpu

F32 = jnp.float32
BF16 = jnp.bfloat16

D_MODEL = 2048
DEPTH = 4
CHUNK = 64
HEAD_DIM = 128
N_MIXERS = 3
A_HEADS = 16
A_KV_HEADS = 4
A_GROUP = A_HEADS // A_KV_HEADS
WINDOW_CHUNKS = 2
BAND = (WINDOW_CHUNKS + 1) * CHUNK
B_HEADS = 8
B_VDIM = 2 * HEAD_DIM
C_HEADS = 16
N_BUCKETS = 32
MAX_DISTANCE = 128
N_GROUPS = 4
EXPERTS_PER_GROUP = 4
N_EXPERTS = N_GROUPS * EXPERTS_PER_GROUP
D_EXPERT = D_MODEL // 4
EPS = 1e-6
QK_SCALE = HEAD_DIM ** -0.5
LOG2E = math.log2(math.e)
ROW_CHUNK = 32

LANES = 128
MASKED = -1e30
VMEM_LIMIT = 56 * 1024 * 1024

NT_DIMS = (((1,), (1,)), ((), ()))


def _cparams(*sem):
    return pltpu.CompilerParams(dimension_semantics=sem, vmem_limit_bytes=VMEM_LIMIT)


def _lambda_init(layer):
    return 0.8 - 0.6 * math.exp(-0.3 * layer)


def _t5_bucket_np(rel):
    half = N_BUCKETS // 2
    max_exact = half // 2
    n = np.abs(rel)
    nf = np.maximum(n, 1).astype(np.float32)
    large = max_exact + (np.log(nf / np.float32(max_exact)) / np.float32(math.log(MAX_DISTANCE / max_exact))
                         * np.float32(half - max_exact)).astype(np.int32)
    large = np.minimum(large, half - 1)
    return np.where(rel > 0, half, 0) + np.where(n < max_exact, n, large)


def _bias_tile(table, q_pos, k_pos, valid):
    b = _t5_bucket_np(k_pos[None, :] - q_pos[:, None])
    t = jnp.moveaxis(table.astype(F32)[jnp.asarray(b)], -1, 0)
    return jnp.where(jnp.asarray(valid)[None], t, MASKED)


def _mod_spec(mod, tm, tn, imap_rows, imap_one):
    if mod.shape[1] == 1:
        return pl.BlockSpec((None, 1, tn), imap_one)
    return pl.BlockSpec((None, tm, tn), imap_rows)


def _ada_kernel(c_ref, w_ref, b_ref, o_ref):
    c = c_ref[...]
    a = (c / (1.0 + jnp.exp(-c))).astype(BF16)
    o_ref[...] = jnp.dot(a, w_ref[...].astype(BF16), preferred_element_type=F32) + b_ref[...]


def _ada_mod(c_all, w_ada, b_ada):
    rows = c_all.shape[0]
    depth, d, n = w_ada.shape
    tn = 1024
    return pl.pallas_call(
        _ada_kernel,
        grid=(depth, n // tn),
        in_specs=[pl.BlockSpec((rows, d), lambda l, j: (0, 0)),
                  pl.BlockSpec((None, d, tn), lambda l, j: (l, 0, j)),
                  pl.BlockSpec((None, 1, tn), lambda l, j: (l, 0, j))],
        out_specs=pl.BlockSpec((None, rows, tn), lambda l, j: (l, 0, j)),
        out_shape=jax.ShapeDtypeStruct((depth, rows, n), F32),
        compiler_params=_cparams("parallel", "parallel"),
        name="ada_mod",
    )(c_all, w_ada, b_ada.reshape(depth, 1, n))


def _norm_mod_f32(x, g, sc, sh):
    y = x * lax.rsqrt(jnp.mean(x * x, axis=-1, keepdims=True) + EPS) * g
    return y * (1.0 + sc) + sh


def _norm_mod_kernel(x_ref, g_ref, sc_ref, sh_ref, o_ref):
    o_ref[...] = _norm_mod_f32(x_ref[...], g_ref[...], sc_ref[...], sh_ref[...]).astype(o_ref.dtype)


def _norm_mod(x, gain, sc, sh):
    b, s, d = x.shape
    tm = min(512, s)
    rows = lambda bb, i: (bb, i, 0)
    one = lambda bb, i: (bb, 0, 0)
    return pl.pallas_call(
        _norm_mod_kernel,
        grid=(b, s // tm),
        in_specs=[pl.BlockSpec((None, tm, d), rows),
                  pl.BlockSpec((1, d), lambda bb, i: (0, 0)),
                  _mod_spec(sc, tm, d, rows, one),
                  _mod_spec(sh, tm, d, rows, one)],
        out_specs=pl.BlockSpec((None, tm, d), rows),
        out_shape=jax.ShapeDtypeStruct((b, s, d), BF16),
        compiler_params=_cparams("parallel", "parallel"),
        name="norm_mod",
    )(x, gain.reshape(1, d), sc, sh)


def _proj_kernel(*refs, has_gain, scale, out32, outbf):
    x_ref, w_ref = refs[0], refs[1]
    pos = 2
    g_ref = None
    if has_gain:
        g_ref = refs[pos]
        pos += 1
    o32_ref = obf_ref = None
    if out32:
        o32_ref = refs[pos]
        pos += 1
    if outbf:
        obf_ref = refs[pos]
    acc = jnp.dot(x_ref[...], w_ref[...], preferred_element_type=F32)
    tn = acc.shape[1]
    for c in range(tn // HEAD_DIM):
        sl = slice(c * HEAD_DIM, (c + 1) * HEAD_DIM)
        t = acc[:, sl]
        if has_gain:
            t = t * lax.rsqrt(jnp.mean(t * t, axis=-1, keepdims=True) + EPS) * g_ref[...]
        if out32:
            o32_ref[:, sl] = t
        if outbf:
            obf_ref[:, sl] = (t * scale).astype(BF16) if scale != 1.0 else t.astype(BF16)


def _proj(x, w, gain=None, scale=1.0, out32=False, outbf=True):
    b, s, k = x.shape
    n = w.shape[1]
    tm = min(1024, s)
    tn = min(1024, n)
    in_specs = [pl.BlockSpec((None, tm, k), lambda j, bb, i: (bb, i, 0)),
                pl.BlockSpec((k, tn), lambda j, bb, i: (0, j))]
    args = [x, w]
    if gain is not None:
        in_specs.append(pl.BlockSpec((1, HEAD_DIM), lambda j, bb, i: (0, 0)))
        args.append(gain.reshape(1, HEAD_DIM).astype(F32))
    out_specs, out_shape = [], []
    for flag, dt in ((out32, F32), (outbf, BF16)):
        if flag:
            out_specs.append(pl.BlockSpec((None, tm, tn), lambda j, bb, i: (bb, i, j)))
            out_shape.append(jax.ShapeDtypeStruct((b, s, n), dt))
    outs = pl.pallas_call(
        functools.partial(_proj_kernel, has_gain=gain is not None, scale=scale, out32=out32, outbf=outbf),
        grid=(n // tn, b, s // tm),
        in_specs=in_specs, out_specs=out_specs, out_shape=out_shape,
        compiler_params=_cparams("parallel", "parallel", "parallel"),
        name="proj",
    )(*args)
    return outs if len(outs) > 1 else outs[0]


def _out_res_kernel(o_ref, w_ref, x_ref, g_ref, y_ref):
    acc = jnp.dot(o_ref[...], w_ref[...], preferred_element_type=F32)
    y_ref[...] = x_ref[...] + g_ref[...] * acc


def _out_res(o, w, x, gate):
    b, s, k = o.shape
    n = w.shape[1]
    tm = min(1024, s)
    tn = min(1024, n)
    rows = lambda j, bb, i: (bb, i, j)
    one = lambda j, bb, i: (bb, 0, j)
    return pl.pallas_call(
        _out_res_kernel,
        grid=(n // tn, b, s // tm),
        in_specs=[pl.BlockSpec((None, tm, k), lambda j, bb, i: (bb, i, 0)),
                  pl.BlockSpec((k, tn), lambda j, bb, i: (0, j)),
                  pl.BlockSpec((None, tm, tn), rows),
                  _mod_spec(gate, tm, tn, rows, one)],
        out_specs=pl.BlockSpec((None, tm, tn), rows),
        out_shape=jax.ShapeDtypeStruct((b, s, n), F32),
        compiler_params=_cparams("parallel", "parallel", "parallel"),
        name="out_res",
    )(o, w, x, gate)


def _sink_attend(q, k_parts, v_parts, bias_parts, sink):
    s_parts = [lax.dot_general(q, kp, NT_DIMS, preferred_element_type=F32) + bp
               for kp, bp in zip(k_parts, bias_parts)]
    m = sink
    for sp in s_parts:
        m = jnp.maximum(m, jnp.max(sp, axis=-1, keepdims=True))
    denom = jnp.exp(sink - m)
    e_parts = []
    for sp in s_parts:
        e = jnp.exp(sp - m)
        denom = denom + jnp.sum(e, axis=-1, keepdims=True)
        e_parts.append(e)
    inv = 1.0 / denom
    o = None
    for e, vp in zip(e_parts, v_parts):
        t = jnp.dot((e * inv).astype(BF16), vp, preferred_element_type=F32)
        o = t if o is None else o + t
    return o


def _attn_a_prompt_kernel(q_ref, kp_ref, k_ref, vp_ref, v_ref, bias_ref, bias0_ref, bias1_ref, sink_ref, o_ref,
                          kc_sc, vc_sc, s_sc, p_sc, *, tq):
    i = pl.program_id(2)
    prev = WINDOW_CHUNKS * CHUNK
    kc_sc[:prev] = kp_ref[...]
    kc_sc[prev:] = k_ref[...]
    vc_sc[:prev] = vp_ref[...]
    vc_sc[prev:] = v_ref[...]
    first = i == 0
    n_chunks = tq // CHUNK

    def scores(c):
        rs = slice(c * CHUNK, (c + 1) * CHUNK)
        qc = jnp.concatenate([q_ref[rs, g * HEAD_DIM:(g + 1) * HEAD_DIM] for g in range(A_GROUP)], axis=0)
        s_sc[c] = lax.dot_general(qc, kc_sc[c * CHUNK:c * CHUNK + BAND], NT_DIMS, preferred_element_type=F32)

    def softmax(c):
        bias = bias_ref[...]
        if c == 0:
            bias = jnp.where(first, bias0_ref[...], bias)
        elif c == 1:
            bias = jnp.where(first, bias1_ref[...], bias)
        s = s_sc[c] + bias
        sink = sink_ref[...]
        m = jnp.maximum(sink, jnp.max(s, axis=-1, keepdims=True))
        e = jnp.exp(s - m)
        inv = 1.0 / (jnp.exp(sink - m) + jnp.sum(e, axis=-1, keepdims=True))
        p_sc[c] = (e * inv).astype(BF16)

    def values(c):
        rs = slice(c * CHUNK, (c + 1) * CHUNK)
        o = jnp.dot(p_sc[c], vc_sc[c * CHUNK:c * CHUNK + BAND], preferred_element_type=F32)
        for g in range(A_GROUP):
            o_ref[rs, g * HEAD_DIM:(g + 1) * HEAD_DIM] = o[g * CHUNK:(g + 1) * CHUNK].astype(o_ref.dtype)

    stages = (scores, softmax, values)
    for tick in range(n_chunks + len(stages) - 1):
        for si, stage in enumerate(stages):
            if 0 <= tick - si < n_chunks:
                stage(tick - si)


def _attn_a_prompt(q, k, v, table, sinks):
    b, s, _ = q.shape
    tq = min(512, s)
    prev = WINDOW_CHUNKS * CHUNK
    r = tq // prev
    k_loc = np.arange(BAND)
    q_loc = prev + np.arange(CHUNK)

    def tiles(first_key):
        valid = np.broadcast_to((k_loc >= first_key)[None, :], (CHUNK, BAND))
        t = _bias_tile(table, q_loc, k_loc, valid)
        return t.reshape(A_KV_HEADS, A_GROUP * CHUNK, BAND)

    bias, bias0, bias1 = tiles(0), tiles(prev), tiles(CHUNK)
    sink_col = jnp.repeat(sinks.astype(F32).reshape(A_KV_HEADS, A_GROUP), CHUNK, axis=1)[..., None]
    qmap = lambda bb, h, i: (bb, i, h)
    pmap = lambda bb, h, i: (bb, jnp.maximum(i * r - 1, 0), h)
    bmap = lambda bb, h, i: (h, 0, 0)
    bspec = pl.BlockSpec((None, A_GROUP * CHUNK, BAND), bmap)
    return pl.pallas_call(
        functools.partial(_attn_a_prompt_kernel, tq=tq),
        grid=(b, A_KV_HEADS, s // tq),
        in_specs=[pl.BlockSpec((None, tq, A_GROUP * HEAD_DIM), qmap),
                  pl.BlockSpec((None, prev, HEAD_DIM), pmap),
                  pl.BlockSpec((None, tq, HEAD_DIM), qmap),
                  pl.BlockSpec((None, prev, HEAD_DIM), pmap),
                  pl.BlockSpec((None, tq, HEAD_DIM), qmap),
                  bspec, bspec, bspec,
                  pl.BlockSpec((None, A_GROUP * CHUNK, 1), bmap)],
        out_specs=pl.BlockSpec((None, tq, A_GROUP * HEAD_DIM), qmap),
        out_shape=jax.ShapeDtypeStruct((b, s, D_MODEL), BF16),
        scratch_shapes=[pltpu.VMEM((prev + tq, HEAD_DIM), BF16), pltpu.VMEM((prev + tq, HEAD_DIM), BF16),
                        pltpu.VMEM((tq // CHUNK, A_GROUP * CHUNK, BAND), F32),
                        pltpu.VMEM((tq // CHUNK, A_GROUP * CHUNK, BAND), BF16)],
        compiler_params=_cparams("parallel", "parallel", "parallel"),
        name="attn_a_prompt",
    )(q, k, k, v, v, bias, bias0, bias1, sink_col)


def _attn_a_sample_kernel(q_ref, kc_ref, kn_ref, vc_ref, vn_ref, bc_ref, bn_ref, sink_ref, o_ref, *, t):
    qs = jnp.concatenate([q_ref[:, g * HEAD_DIM:(g + 1) * HEAD_DIM] for g in range(A_GROUP)], axis=0)
    o = _sink_attend(qs,
                     [kc_ref[...].astype(BF16), kn_ref[...].astype(BF16)],
                     [vc_ref[...].astype(BF16), vn_ref[...].astype(BF16)],
                     [bc_ref[...], bn_ref[...]], sink_ref[...])
    for g in range(A_GROUP):
        o_ref[:, g * HEAD_DIM:(g + 1) * HEAD_DIM] = o[g * t:(g + 1) * t].astype(o_ref.dtype)


def _attn_a_sample(q, ck, cv, kn, vn, table, sinks, past_len):
    b, t, _ = q.shape
    cache = ck.shape[1]
    q_pos = past_len + np.arange(t)
    k_pos = np.concatenate([past_len - cache + np.arange(cache), q_pos])
    qc, kc = q_pos[:, None] // CHUNK, k_pos[None, :] // CHUNK
    valid = (kc <= qc) & (kc >= qc - WINDOW_CHUNKS)
    bias = _bias_tile(table, q_pos, k_pos, valid).reshape(A_KV_HEADS, A_GROUP * t, cache + t)
    sink_col = jnp.repeat(sinks.astype(F32).reshape(A_KV_HEADS, A_GROUP), t, axis=1)[..., None]
    hmap = lambda bb, h: (bb, 0, h)
    bmap = lambda bb, h: (h, 0, 0)
    return pl.pallas_call(
        functools.partial(_attn_a_sample_kernel, t=t),
        grid=(b, A_KV_HEADS),
        in_specs=[pl.BlockSpec((None, t, A_GROUP * HEAD_DIM), hmap),
                  pl.BlockSpec((None, cache, HEAD_DIM), hmap),
                  pl.BlockSpec((None, t, HEAD_DIM), hmap),
                  pl.BlockSpec((None, cache, HEAD_DIM), hmap),
                  pl.BlockSpec((None, t, HEAD_DIM), hmap),
                  pl.BlockSpec((None, A_GROUP * t, cache), bmap),
                  pl.BlockSpec((None, A_GROUP * t, t), bmap),
                  pl.BlockSpec((None, A_GROUP * t, 1), bmap)],
        out_specs=pl.BlockSpec((None, t, A_GROUP * HEAD_DIM), hmap),
        out_shape=jax.ShapeDtypeStruct((b, t, D_MODEL), BF16),
        compiler_params=_cparams("parallel", "parallel"),
        name="attn_a_sample",
    )(q, ck, kn, cv, vn, bias[..., :cache], bias[..., cache:], sink_col)


def _diff_lambda(lam_ref, lam_init):
    lp = lam_ref[...]
    a = jnp.sum(lp[0:1] * lp[1:2], axis=-1, keepdims=True)
    c = jnp.sum(lp[2:3] * lp[3:4], axis=-1, keepdims=True)
    return jnp.exp(a) - jnp.exp(c) + lam_init


def _finish_b(o0, o1, lam, gsub, lam_init):
    o = o0 - lam * o1
    o = o * lax.rsqrt(jnp.mean(o * o, axis=-1, keepdims=True) + EPS) * gsub
    return o * (1.0 - lam_init)


def _attn_b_prompt_kernel(q_ref, k_ref, v_ref, bias_ref, lam_ref, gsub_ref, o_ref,
                          s0_sc, s1_sc, p0_sc, p1_sc, al_sc, m_sc, l_sc, acc_sc, *, t, lam_init):
    i = pl.program_id(2)
    s_scs = (s0_sc, s1_sc)
    p_scs = (p0_sc, p1_sc)
    m_sc[...] = jnp.full(m_sc.shape, MASKED, F32)
    l_sc[...] = jnp.zeros(l_sc.shape, F32)
    acc_sc[...] = jnp.zeros(acc_sc.shape, F32)

    def keys(j):
        return pl.ds(pl.multiple_of(j * t, t), t)

    def scores(j, mp):
        sl = slice(mp * HEAD_DIM, (mp + 1) * HEAD_DIM)
        s_scs[mp][...] = jnp.dot(q_ref[:, sl], k_ref[sl, keys(j)], preferred_element_type=F32)

    def softmax(j, mp):
        kind = jnp.minimum(i - j, 2)
        s = s_scs[mp][...] + bias_ref[mp, kind]
        m_old = m_sc[mp]
        m_new = jnp.maximum(m_old, jnp.max(s, axis=-1, keepdims=True))
        alpha = jnp.exp2(m_old - m_new)
        p = jnp.exp2(s - m_new)
        l_sc[mp] = alpha * l_sc[mp] + jnp.sum(p, axis=-1, keepdims=True)
        m_sc[mp] = m_new
        al_sc[mp] = alpha
        p_scs[mp][...] = p.astype(BF16)

    def weigh(j, mp):
        acc_sc[mp] = al_sc[mp] * acc_sc[mp] + jnp.dot(p_scs[mp][...], v_ref[keys(j), :],
                                                      preferred_element_type=F32)

    scores(0, 0)
    scores(0, 1)
    softmax(0, 0)

    def body(j, carry):
        scores(j, 0)
        softmax(j - 1, 1)
        weigh(j - 1, 0)
        scores(j, 1)
        softmax(j, 0)
        weigh(j - 1, 1)
        return carry

    lax.fori_loop(1, i + 1, body, 0)
    softmax(i, 1)
    weigh(i, 0)
    weigh(i, 1)
    lam = _diff_lambda(lam_ref, lam_init)
    o = _finish_b(acc_sc[0] / l_sc[0], acc_sc[1] / l_sc[1], lam, gsub_ref[...], lam_init)
    o_ref[...] = o.astype(o_ref.dtype)


def _toeplitz(f, t):
    h = f.shape[0]
    g = jnp.pad(f, ((0, 0), (0, 1)))
    flat = jnp.tile(g, (1, t))[:, :t * (2 * t - 1)]
    return flat.reshape(h, t, 2 * t - 1)[:, :, t - 1:]


def _attn_b_prompt(q, k, v, table, lam_rows, gsub, lam_init):
    b, s, _ = q.shape
    t = min(512, s)
    loc = np.arange(t)
    chunk_ok = jnp.asarray((loc[None, :] // CHUNK) <= (loc[:, None] // CHUNK))
    tab = table.astype(F32) * LOG2E
    rel = np.arange(-(t - 1), t)
    diag = _toeplitz(tab[jnp.asarray(_t5_bucket_np(rel))].T, t)
    diag = jnp.where(chunk_ok[None], diag, MASKED)
    sub = _toeplitz(tab[jnp.asarray(_t5_bucket_np(rel - t))].T, t)
    far_rel = -(t + 1 + np.arange(max(s - t, 1)))
    far_bucket = _t5_bucket_np(far_rel)
    assert np.all(far_bucket == far_bucket[0])
    far = jnp.broadcast_to(tab[int(far_bucket[0])][:, None, None], (2 * B_HEADS, t, t))
    bias = jnp.stack([diag, sub, far], axis=1)
    qmap = lambda bb, h, i: (bb, i, h)
    kmap = lambda bb, h, i: (bb, 0, h)
    return pl.pallas_call(
        functools.partial(_attn_b_prompt_kernel, t=t, lam_init=lam_init),
        grid=(b, B_HEADS, s // t),
        in_specs=[pl.BlockSpec((None, t, B_VDIM), qmap),
                  pl.BlockSpec((None, B_VDIM, s), lambda bb, h, i: (bb, h, 0)),
                  pl.BlockSpec((None, s, B_VDIM), kmap),
                  pl.BlockSpec((2, 3, t, t), lambda bb, h, i: (h, 0, 0, 0)),
                  pl.BlockSpec((4, HEAD_DIM), lambda bb, h, i: (0, 0)),
                  pl.BlockSpec((1, B_VDIM), lambda bb, h, i: (0, 0))],
        out_specs=pl.BlockSpec((None, t, B_VDIM), qmap),
        out_shape=jax.ShapeDtypeStruct((b, s, D_MODEL), BF16),
        scratch_shapes=[pltpu.VMEM((t, t), F32), pltpu.VMEM((t, t), F32),
                        pltpu.VMEM((t, t), BF16), pltpu.VMEM((t, t), BF16),
                        pltpu.VMEM((2, t, 1), F32), pltpu.VMEM((2, t, 1), F32), pltpu.VMEM((2, t, 1), F32),
                        pltpu.VMEM((2, t, B_VDIM), F32)],
        compiler_params=_cparams("parallel", "parallel", "parallel"),
        name="attn_b_prompt",
    )(q, jnp.swapaxes(k, 1, 2), v, bias, lam_rows, gsub.reshape(1, B_VDIM).astype(F32))


def _attn_b_sample_kernel(q_ref, kc_ref, kn_ref, vc_ref, vn_ref, bc_ref, bn_ref, lam_ref, gsub_ref, o_ref,
                          *, lam_init):
    kc = kc_ref[...].astype(BF16)
    kn = kn_ref[...].astype(BF16)
    vc = vc_ref[...].astype(BF16)
    vn = vn_ref[...].astype(BF16)
    outs = []
    for mp in range(2):
        sl = slice(mp * HEAD_DIM, (mp + 1) * HEAD_DIM)
        qm = q_ref[:, sl]
        sc = lax.dot_general(qm, kc[:, sl], NT_DIMS, preferred_element_type=F32) + bc_ref[mp]
        sn = lax.dot_general(qm, kn[:, sl], NT_DIMS, preferred_element_type=F32) + bn_ref[mp]
        m = jnp.maximum(jnp.max(sc, axis=-1, keepdims=True), jnp.max(sn, axis=-1, keepdims=True))
        pc = jnp.exp(sc - m)
        pn = jnp.exp(sn - m)
        inv = 1.0 / (jnp.sum(pc, axis=-1, keepdims=True) + jnp.sum(pn, axis=-1, keepdims=True))
        outs.append(jnp.dot((pc * inv).astype(BF16), vc, preferred_element_type=F32)
                    + jnp.dot((pn * inv).astype(BF16), vn, preferred_element_type=F32))
    lam = _diff_lambda(lam_ref, lam_init)
    o_ref[...] = _finish_b(outs[0], outs[1], lam, gsub_ref[...], lam_init).astype(o_ref.dtype)


def _attn_b_sample(q, ck, cv, kn, vn, table, lam_rows, gsub, lam_init, past_len):
    b, t, _ = q.shape
    past = ck.shape[1]
    q_pos = past_len + np.arange(t)
    k_pos = np.arange(past_len + t)
    valid = (k_pos[None, :] // CHUNK) <= (q_pos[:, None] // CHUNK)
    bias = _bias_tile(table, q_pos, k_pos, valid).reshape(B_HEADS, 2, t, past + t)
    hmap = lambda bb, h: (bb, 0, h)
    bmap = lambda bb, h: (h, 0, 0, 0)
    return pl.pallas_call(
        functools.partial(_attn_b_sample_kernel, lam_init=lam_init),
        grid=(b, B_HEADS),
        in_specs=[pl.BlockSpec((None, t, B_VDIM), hmap),
                  pl.BlockSpec((None, past, B_VDIM), hmap),
                  pl.BlockSpec((None, t, B_VDIM), hmap),
                  pl.BlockSpec((None, past, B_VDIM), hmap),
                  pl.BlockSpec((None, t, B_VDIM), hmap),
                  pl.BlockSpec((None, 2, t, past), bmap),
                  pl.BlockSpec((None, 2, t, t), bmap),
                  pl.BlockSpec((4, HEAD_DIM), lambda bb, h: (0, 0)),
                  pl.BlockSpec((1, B_VDIM), lambda bb, h: (0, 0))],
        out_specs=pl.BlockSpec((None, t, B_VDIM), hmap),
        out_shape=jax.ShapeDtypeStruct((b, t, D_MODEL), BF16),
        compiler_params=_cparams("parallel", "parallel"),
        name="attn_b_sample",
    )(q, ck, kn, cv, vn, bias[..., :past], bias[..., past:], lam_rows, gsub.reshape(1, B_VDIM).astype(F32))


def _neg_abs(x):
    bits = lax.bitcast_convert_type(x, jnp.uint32) | jnp.uint32(0x80000000)
    return lax.bitcast_convert_type(bits, F32)


def _minus_later_ones(n):
    idx = np.arange(n)
    return jnp.asarray(-(idx[:, None] > idx[None, :]).astype(np.float32), dtype=BF16)


def _sb_block(q, kb, vb, u, run, mask):
    z = lax.dot_general(q, kb, NT_DIMS, preferred_element_type=F32)
    sp = jnp.maximum(z, 0.0) + jnp.log(1.0 + jnp.exp(-jnp.abs(z)))
    if mask is not None:
        sp = jnp.where(mask, sp, 0.0)
    after = jnp.dot(sp.astype(BF16), u, preferred_element_type=F32) + run
    a = jnp.exp((z - sp) + after)
    if mask is not None:
        a = jnp.where(mask, a, 0.0)
    o = jnp.dot(a.astype(BF16), vb, preferred_element_type=F32)
    return o, run - jnp.sum(sp, axis=-1, keepdims=True)


C_HEADS_PER_STEP = 4


def _attn_c_prompt_kernel(q_ref, k_ref, v_ref, u_ref, o_ref, z_sc, spb_sc, aft_sc, a_sc, off_sc, run_sc, acc_sc,
                          *, t):
    i = pl.program_id(2)
    hp = C_HEADS_PER_STEP
    heads = [slice(h * HEAD_DIM, (h + 1) * HEAD_DIM) for h in range(hp)]
    run_sc[...] = jnp.zeros(run_sc.shape, F32)
    acc_sc[...] = jnp.zeros(acc_sc.shape, F32)

    def block(start, masked):
        keys = pl.ds(start, t)
        if masked:
            visible = lax.broadcasted_iota(jnp.int32, (t, t), 1) < lax.broadcasted_iota(jnp.int32, (t, t), 0)

        def scores(h):
            z_sc[h] = lax.dot_general(q_ref[:, heads[h]], k_ref[keys, heads[h]], NT_DIMS,
                                      preferred_element_type=F32)

        def softplus(h):
            z = z_sc[h]
            sp = jnp.maximum(z, 0.0) + jnp.log(1.0 + jnp.exp(_neg_abs(z)))
            if masked:
                sp = jnp.where(visible, sp, 0.0)
            z_sc[h] = z - sp
            spb_sc[h] = sp.astype(BF16)
            run = run_sc[h]
            off_sc[h] = jnp.exp(run)
            run_sc[h] = run - jnp.sum(sp, axis=-1, keepdims=True)

        def later_sums(h):
            aft_sc[h] = jnp.dot(spb_sc[h], u_ref[...], preferred_element_type=F32)

        def weights(h):
            a = jnp.exp(z_sc[h] + aft_sc[h])
            if masked:
                a = jnp.where(visible, a, 0.0)
            a_sc[h] = a.astype(BF16)

        def values(h):
            acc_sc[h] += off_sc[h] * jnp.dot(a_sc[h], v_ref[keys, heads[h]], preferred_element_type=F32)

        stages = (scores, softplus, later_sums, weights, values)
        for tick in range(hp + len(stages) - 1):
            for si, stage in enumerate(stages):
                if 0 <= tick - si < hp:
                    stage(tick - si)

    block(pl.multiple_of(i * t, t), True)

    def body(n, carry):
        block(pl.multiple_of((i - 1 - n) * t, t), False)
        return carry

    lax.fori_loop(0, i, body, 0)
    for h in range(hp):
        o_ref[:, heads[h]] = acc_sc[h].astype(o_ref.dtype)


def _attn_c_prompt(q, k, v):
    b, s, _ = q.shape
    t = min(256, s)
    hp = C_HEADS_PER_STEP
    qmap = lambda bb, h, i: (bb, i, h)
    kmap = lambda bb, h, i: (bb, 0, h)
    return pl.pallas_call(
        functools.partial(_attn_c_prompt_kernel, t=t),
        grid=(b, C_HEADS // hp, s // t),
        in_specs=[pl.BlockSpec((None, t, hp * HEAD_DIM), qmap),
                  pl.BlockSpec((None, s, hp * HEAD_DIM), kmap),
                  pl.BlockSpec((None, s, hp * HEAD_DIM), kmap),
                  pl.BlockSpec((t, t), lambda bb, h, i: (0, 0))],
        out_specs=pl.BlockSpec((None, t, hp * HEAD_DIM), qmap),
        out_shape=jax.ShapeDtypeStruct((b, s, D_MODEL), BF16),
        scratch_shapes=[pltpu.VMEM((hp, t, t), F32), pltpu.VMEM((hp, t, t), BF16), pltpu.VMEM((hp, t, t), F32),
                        pltpu.VMEM((hp, t, t), BF16), pltpu.VMEM((hp, t, 1), F32),
                        pltpu.VMEM((hp, t, 1), F32), pltpu.VMEM((hp, t, HEAD_DIM), F32)],
        compiler_params=_cparams("parallel", "parallel", "parallel"),
        name="attn_c_prompt",
    )(q, k, v, _minus_later_ones(t))


def _attn_c_sample_kernel(q_ref, kc_ref, kn_ref, vc_ref, vn_ref, u_ref, o_ref, *, t, past, tk):
    q = q_ref[...]
    u = u_ref[...]
    row = lax.broadcasted_iota(jnp.int32, (t, t), 0)
    col = lax.broadcasted_iota(jnp.int32, (t, t), 1)
    acc, run = _sb_block(q, kn_ref[...].astype(BF16), vn_ref[...].astype(BF16), u[:t, :t],
                         jnp.zeros((t, 1), F32), col < row)
    for j in reversed(range(past // tk)):
        rs = slice(j * tk, (j + 1) * tk)
        o, run = _sb_block(q, kc_ref[rs, :].astype(BF16), vc_ref[rs, :].astype(BF16), u, run, None)
        acc = acc + o
    o_ref[...] = acc.astype(o_ref.dtype)


def _attn_c_sample(q, ck, cv, kn, vn):
    b, t, _ = q.shape
    past = ck.shape[1]
    tk = min(256, past)
    hmap = lambda bb, h: (bb, 0, h)
    return pl.pallas_call(
        functools.partial(_attn_c_sample_kernel, t=t, past=past, tk=tk),
        grid=(b, C_HEADS),
        in_specs=[pl.BlockSpec((None, t, HEAD_DIM), hmap),
                  pl.BlockSpec((None, past, HEAD_DIM), hmap),
                  pl.BlockSpec((None, t, HEAD_DIM), hmap),
                  pl.BlockSpec((None, past, HEAD_DIM), hmap),
                  pl.BlockSpec((None, t, HEAD_DIM), hmap),
                  pl.BlockSpec((tk, tk), lambda bb, h: (0, 0))],
        out_specs=pl.BlockSpec((None, t, HEAD_DIM), hmap),
        out_shape=jax.ShapeDtypeStruct((b, t, D_MODEL), BF16),
        compiler_params=_cparams("parallel", "parallel"),
        name="attn_c_sample",
    )(q, ck, kn, cv, vn, _minus_later_ones(tk))


ROUTE_LANES = LANES
EXPERT_LANE0 = N_GROUPS
ROW_LANES = LANES
ROW_TILE = D_MODEL // ROW_LANES


def _store_row_tiled(ref, val):
    rows = val.shape[0]
    for s in range(ROW_TILE):
        ref[pl.ds(s, rows, stride=ROW_TILE), :] = val[:, s * ROW_LANES:(s + 1) * ROW_LANES]


def _load_row_tiled(ref, rows, s, base=0):
    return ref[pl.ds(base * ROW_TILE + s, rows, stride=ROW_TILE), :]


def _router_kernel(x_ref, g_ref, sc_ref, sh_ref, w_ref, b_ref, h_ref, r_ref):
    h = _norm_mod_f32(x_ref[...], g_ref[...], sc_ref[...], sh_ref[...])
    _store_row_tiled(h_ref, h)
    h_hi = h.astype(BF16)
    h_lo = (h - h_hi.astype(F32)).astype(BF16)
    w = w_ref[...]
    both = jnp.dot(h_hi, w, preferred_element_type=F32)
    logits = (both[:, :ROUTE_LANES] + both[:, ROUTE_LANES:]
              + jnp.dot(h_lo, w[:, :ROUTE_LANES], preferred_element_type=F32) + b_ref[...])
    lane = lax.broadcasted_iota(jnp.int32, logits.shape, 1)
    lanef = lane.astype(F32)

    def first_lane_of_max(v, vmax):
        return jnp.min(jnp.where(v == vmax, lanef, float(ROUTE_LANES)), axis=-1, keepdims=True)

    lg = jnp.where(lane < N_GROUPS, logits, MASKED)
    mg = jnp.max(lg, axis=-1, keepdims=True)
    gate = 1.0 / jnp.sum(jnp.exp(lg - mg), axis=-1, keepdims=True)
    gi = first_lane_of_max(lg, mg)
    lo = EXPERT_LANE0 + EXPERTS_PER_GROUP * gi
    le = jnp.where((lanef >= lo) & (lanef < lo + EXPERTS_PER_GROUP), logits, MASKED)
    v1 = jnp.max(le, axis=-1, keepdims=True)
    i1 = first_lane_of_max(le, v1)
    le2 = jnp.where(lanef == i1, MASKED, le)
    v2 = jnp.max(le2, axis=-1, keepdims=True)
    i2 = first_lane_of_max(le2, v2)
    e21 = jnp.exp(v2 - v1)
    w1 = gate / (1.0 + e21)
    w2 = w1 * e21
    r_ref[...] = jnp.where(lane == 0, i1 - EXPERT_LANE0,
                           jnp.where(lane == 1, i2 - EXPERT_LANE0,
                                     jnp.where(lane == 2, w1, jnp.where(lane == 3, w2, 0.0))))


def _router(x, gain, sc, sh, w_cat, b_row):
    b, s, d = x.shape
    tm = min(512, s)
    rows = lambda bb, i: (bb, i, 0)
    one = lambda bb, i: (bb, 0, 0)
    return pl.pallas_call(
        _router_kernel,
        grid=(b, s // tm),
        in_specs=[pl.BlockSpec((None, tm, d), rows),
                  pl.BlockSpec((1, d), lambda bb, i: (0, 0)),
                  _mod_spec(sc, tm, d, rows, one),
                  _mod_spec(sh, tm, d, rows, one),
                  pl.BlockSpec((d, 2 * ROUTE_LANES), lambda bb, i: (0, 0)),
                  pl.BlockSpec((1, ROUTE_LANES), lambda bb, i: (0, 0))],
        out_specs=[pl.BlockSpec((None, tm * ROW_TILE, ROW_LANES), rows),
                   pl.BlockSpec((None, tm, ROUTE_LANES), rows)],
        out_shape=[jax.ShapeDtypeStruct((b, s * ROW_TILE, ROW_LANES), F32),
                   jax.ShapeDtypeStruct((b, s, ROUTE_LANES), F32)],
        compiler_params=_cparams("parallel", "parallel"),
        name="moe_router",
    )(x, gain.reshape(1, d), sc, sh, w_cat, b_row)


def _router_weights(w_group, b_group, w_router, b_router):
    w = jnp.concatenate([w_group, jnp.moveaxis(w_router, 0, 1).reshape(D_MODEL, N_EXPERTS)], axis=1)
    w = jnp.pad(w.astype(F32), ((0, 0), (0, ROUTE_LANES - w.shape[1])))
    hi = w.astype(BF16)
    lo = (w - hi.astype(F32)).astype(BF16)
    bias = jnp.concatenate([b_group, b_router.reshape(N_EXPERTS)]).astype(F32)
    bias = jnp.pad(bias, (0, ROUTE_LANES - bias.shape[0])).reshape(1, ROUTE_LANES)
    return jnp.concatenate([hi, lo], axis=1), bias


def _row_gather(src_hbm, idx_ref, n_rows, dst, sem, inline=False, alternate=False):
    def start(r):
        src = pl.multiple_of(idx_ref[0, r] * ROW_TILE, ROW_TILE)
        static = isinstance(r, int)
        dst_rows = r * ROW_TILE if static else pl.multiple_of(r * ROW_TILE, ROW_TILE)
        pltpu.make_async_copy(src_hbm.at[pl.ds(src, ROW_TILE)], dst.at[pl.ds(dst_rows, ROW_TILE)],
                              sem).start(priority=r % 2 if (static and alternate) else 0)

    if inline:
        for r in range(n_rows):
            start(r)
    else:
        def body(r, carry):
            start(r)
            return carry
        lax.fori_loop(0, n_rows, body, 0, unroll=8)


def _row_gather_wait(src_hbm, n_rows, dst, sem):
    pltpu.make_async_copy(src_hbm.at[pl.ds(0, n_rows * ROW_TILE)], dst, sem).wait()


def _expert_kernel(te_ref, idx_ref, idx_next_ref, h_hbm, wgu_ref, wd_ref, y_ref, buf0, buf1, sem, *, tm):
    i = pl.program_id(0)
    n = pl.num_programs(0)

    @pl.when(i == 0)
    def _():
        _row_gather(h_hbm, idx_ref, tm, buf0, sem.at[0])

    def tile(cur, cur_sem, nxt, nxt_sem):
        _row_gather_wait(h_hbm, tm, cur, cur_sem)
        _row_gather(h_hbm, idx_next_ref, tm, nxt, nxt_sem, inline=True)
        x = jnp.concatenate([_load_row_tiled(cur, tm, s).astype(BF16) for s in range(ROW_TILE)], axis=1)
        gu = jnp.dot(x, wgu_ref[...], preferred_element_type=F32)
        gate, up = gu[:, :D_EXPERT], gu[:, D_EXPERT:]
        hid = (gate / (1.0 + jnp.exp(-gate)) * up).astype(BF16)
        _store_row_tiled(y_ref, jnp.dot(hid, wd_ref[...], preferred_element_type=F32))

        @pl.when(i == n - 1)
        def _():
            _row_gather_wait(h_hbm, tm, nxt, nxt_sem)

    @pl.when(i % 2 == 0)
    def _():
        tile(buf0, sem.at[0], buf1, sem.at[1])

    @pl.when(i % 2 == 1)
    def _():
        tile(buf1, sem.at[1], buf0, sem.at[0])


def _experts(h_rt, tile_expert, src_rows, w_gu, w_down, tm):
    n_tiles = tile_expert.shape[0]
    d = D_MODEL
    grid_spec = pltpu.PrefetchScalarGridSpec(
        num_scalar_prefetch=1,
        grid=(n_tiles,),
        in_specs=[pl.BlockSpec((None, 1, tm), lambda i, te: (i, 0, 0), memory_space=pltpu.SMEM),
                  pl.BlockSpec((None, 1, tm), lambda i, te: (jnp.minimum(i + 1, n_tiles - 1), 0, 0),
                               memory_space=pltpu.SMEM),
                  pl.BlockSpec(memory_space=pl.ANY),
                  pl.BlockSpec((None, d, 2 * D_EXPERT), lambda i, te: (te[i], 0, 0)),
                  pl.BlockSpec((None, D_EXPERT, d), lambda i, te: (te[i], 0, 0))],
        out_specs=pl.BlockSpec((tm * ROW_TILE, ROW_LANES), lambda i, te: (i, 0)),
        scratch_shapes=[pltpu.VMEM((tm * ROW_TILE, ROW_LANES), F32), pltpu.VMEM((tm * ROW_TILE, ROW_LANES), F32),
                        pltpu.SemaphoreType.DMA((2,))],
    )
    idx3 = src_rows.reshape(n_tiles, 1, tm)
    return pl.pallas_call(
        functools.partial(_expert_kernel, tm=tm),
        grid_spec=grid_spec,
        out_shape=jax.ShapeDtypeStruct((n_tiles * tm * ROW_TILE, ROW_LANES), F32),
        compiler_params=_cparams("arbitrary"),
        name="moe_experts",
    )(tile_expert, idx3, idx3, h_rt, w_gu, w_down)


def _combine_kernel(pos_ref, pos_next_ref, y_hbm, x_ref, g_ref, r_ref, o_ref, buf0, buf1, sem, *, tc):
    i = pl.program_id(0) * pl.num_programs(1) + pl.program_id(1)
    n = pl.num_programs(0) * pl.num_programs(1)

    @pl.when(i == 0)
    def _():
        _row_gather(y_hbm, pos_ref, 2 * tc, buf0, sem.at[0])

    def tile(cur, cur_sem, nxt, nxt_sem):
        _row_gather_wait(y_hbm, 2 * tc, cur, cur_sem)
        _row_gather(y_hbm, pos_next_ref, 2 * tc, nxt, nxt_sem, inline=True, alternate=True)
        w1 = r_ref[:, 2:3]
        w2 = r_ref[:, 3:4]
        for s in range(ROW_TILE):
            sl = slice(s * ROW_LANES, (s + 1) * ROW_LANES)
            y = w1 * _load_row_tiled(cur, tc, s) + w2 * _load_row_tiled(cur, tc, s, base=tc)
            o_ref[:, sl] = x_ref[:, sl] + g_ref[:, sl] * y

        @pl.when(i == n - 1)
        def _():
            _row_gather_wait(y_hbm, 2 * tc, nxt, nxt_sem)

    @pl.when(i % 2 == 0)
    def _():
        tile(buf0, sem.at[0], buf1, sem.at[1])

    @pl.when(i % 2 == 1)
    def _():
        tile(buf1, sem.at[1], buf0, sem.at[0])


def _combine(x, y_rt, pos, gate, route, tc):
    b, s, d = x.shape
    nt = s // tc
    n = b * nt
    pos3 = pos.reshape(n, tc, 2).transpose(0, 2, 1).reshape(n, 1, 2 * tc)
    rows = lambda bb, i: (bb, i, 0)
    one = lambda bb, i: (bb, 0, 0)
    return pl.pallas_call(
        functools.partial(_combine_kernel, tc=tc),
        grid=(b, nt),
        in_specs=[pl.BlockSpec((None, 1, 2 * tc), lambda bb, i: (bb * nt + i, 0, 0), memory_space=pltpu.SMEM),
                  pl.BlockSpec((None, 1, 2 * tc), lambda bb, i: (jnp.minimum(bb * nt + i + 1, n - 1), 0, 0),
                               memory_space=pltpu.SMEM),
                  pl.BlockSpec(memory_space=pl.ANY),
                  pl.BlockSpec((None, tc, d), rows),
                  _mod_spec(gate, tc, d, rows, one),
                  pl.BlockSpec((None, tc, ROUTE_LANES), rows)],
        out_specs=pl.BlockSpec((None, tc, d), rows),
        out_shape=jax.ShapeDtypeStruct((b, s, d), F32),
        scratch_shapes=[pltpu.VMEM((2 * tc * ROW_TILE, ROW_LANES), F32),
                        pltpu.VMEM((2 * tc * ROW_TILE, ROW_LANES), F32), pltpu.SemaphoreType.DMA((2,))],
        compiler_params=_cparams("arbitrary", "arbitrary"),
        name="moe_combine",
    )(pos3, pos3, y_rt, x, gate, route)


def _moe(x, gain, sc, sh, gate, wts):
    w_cat, b_row, w_gu, w_down = wts
    b, s, d = x.shape
    tokens = b * s
    h, route = _router(x, gain, sc, sh, w_cat, b_row)
    ids = route.reshape(tokens, ROUTE_LANES)[:, 0:2].astype(jnp.int32).reshape(-1)
    tm = 256 if tokens >= 4096 else 128
    n_assign = 2 * tokens
    n_tiles = (n_assign + N_EXPERTS * (tm - 1)) // tm + 1
    onehot = (ids[:, None] == jnp.arange(N_EXPERTS)[None, :]).astype(jnp.int32)
    csum = jnp.cumsum(onehot, axis=0)
    counts = csum[-1]
    padded = ((counts + tm - 1) // tm) * tm
    ends = jnp.cumsum(padded)
    pos = jnp.sum(onehot * (csum - 1 + (ends - padded)[None, :]), axis=1)
    tile_expert = jnp.minimum(jnp.searchsorted(ends // tm, jnp.arange(n_tiles), side='right'),
                              N_EXPERTS - 1).astype(jnp.int32)
    src_rows = jnp.zeros((n_tiles * tm,), jnp.int32).at[pos].set(jnp.arange(n_assign, dtype=jnp.int32) // 2)
    y = _experts(h.reshape(tokens * ROW_TILE, ROW_LANES), tile_expert, src_rows, w_gu, w_down, tm)
    return _combine(x, y, pos.reshape(tokens, 2), gate, route, min(256, s))


def _split_mod(mod):
    return [mod[..., j * D_MODEL:(j + 1) * D_MODEL] for j in range(6)]


def kernel(x_prompt, x_sample, c_prompt, c_sample, cache_a_k, cache_a_v, cache_b_k, cache_b_v, cache_c_k, cache_c_v, rel_bias_table, norm_mix, norm_ffn, w_ada, b_ada, w_in_a, q_gain_a, k_gain_a, sinks_a, w_out_a, w_in_b, q_gain_b, k_gain_b, lam_q1, lam_k1, lam_q2, lam_k2, sub_gain_b, w_out_b, w_in_c, w_out_c, w_group, b_group, w_router, b_router, w_gate, w_up, w_down):
    nb, seq, d = x_prompt.shape
    db, dt, _ = x_sample.shape
    past_len = cache_b_k.shape[2]
    a_cache = cache_a_k.shape[2]
    ns = db * dt

    n_c = nb + db
    c_rows = -(-n_c // 16) * 16
    c_all = jnp.pad(jnp.concatenate([c_prompt, c_sample], axis=0).astype(F32), ((0, c_rows - n_c), (0, 0)))
    mods = _ada_mod(c_all, w_ada, b_ada)

    xp = x_prompt
    xs = x_sample.reshape(1, ns, d)
    st_p, st_s = [], []
    for l in range(DEPTH):
        i, kind = l // N_MIXERS, l % N_MIXERS
        mp = _split_mod(mods[l, :nb][:, None, :])
        ms = _split_mod(jnp.repeat(mods[l, nb:n_c], dt, axis=0)[None])
        hp = _norm_mod(xp, norm_mix[l], mp[1], mp[0])
        hs = _norm_mod(xs, norm_mix[l], ms[1], ms[0])

        if kind == 0:
            nq, nk = A_HEADS * HEAD_DIM, A_KV_HEADS * HEAD_DIM
            wq = w_in_a[i][:, :nq].astype(BF16)
            wk = w_in_a[i][:, nq:nq + nk].astype(BF16)
            wv = w_in_a[i][:, nq + nk:].astype(BF16)
            wo = w_out_a[i].astype(BF16)
            q = _proj(hp, wq, q_gain_a[i], QK_SCALE)
            k32, kbf = _proj(hp, wk, k_gain_a[i], out32=True)
            v32, vbf = _proj(hp, wv, out32=True)
            o = _attn_a_prompt(q, kbf, vbf, rel_bias_table, sinks_a[i])
            xp = _out_res(o, wo, xp, mp[2])
            st_p.append((k32[:, -a_cache:].reshape(nb, a_cache, A_KV_HEADS, HEAD_DIM),
                         v32[:, -a_cache:].reshape(nb, a_cache, A_KV_HEADS, HEAD_DIM)))
            q = _proj(hs, wq, q_gain_a[i], QK_SCALE).reshape(db, dt, nq)
            kn = _proj(hs, wk, k_gain_a[i], out32=True, outbf=False).reshape(db, dt, nk)
            vn = _proj(hs, wv, out32=True, outbf=False).reshape(db, dt, nk)
            ck = cache_a_k[i].reshape(db, a_cache, nk)
            cv = cache_a_v[i].reshape(db, a_cache, nk)
            o = _attn_a_sample(q, ck, cv, kn, vn, rel_bias_table, sinks_a[i], past_len)
            xs = _out_res(o.reshape(1, ns, d), wo, xs, ms[2])
            kk = jnp.concatenate([ck, kn], axis=1)[:, -a_cache:]
            vv = jnp.concatenate([cv, vn], axis=1)[:, -a_cache:]
            st_s.append((kk.reshape(db, a_cache, A_KV_HEADS, HEAD_DIM),
                         vv.reshape(db, a_cache, A_KV_HEADS, HEAD_DIM)))
        elif kind == 1:
            nq = B_HEADS * 2 * HEAD_DIM
            lam_init = _lambda_init(l)
            wq = w_in_b[i][:, :nq].astype(BF16)
            wk = w_in_b[i][:, nq:2 * nq].astype(BF16)
            wv = w_in_b[i][:, 2 * nq:].astype(BF16)
            wo = w_out_b[i].astype(BF16)
            lam_rows = jnp.stack([lam_q1[i], lam_k1[i], lam_q2[i], lam_k2[i]]).astype(F32)
            q = _proj(hp, wq, q_gain_b[i], QK_SCALE * LOG2E)
            k32, kbf = _proj(hp, wk, k_gain_b[i], out32=True)
            v32, vbf = _proj(hp, wv, out32=True)
            o = _attn_b_prompt(q, kbf, vbf, rel_bias_table, lam_rows, sub_gain_b[i], lam_init)
            xp = _out_res(o, wo, xp, mp[2])
            st_p.append((k32.reshape(nb, seq, B_HEADS, 2, HEAD_DIM), v32.reshape(nb, seq, B_HEADS, B_VDIM)))
            q = _proj(hs, wq, q_gain_b[i], QK_SCALE).reshape(db, dt, nq)
            kn = _proj(hs, wk, k_gain_b[i], out32=True, outbf=False).reshape(db, dt, nq)
            vn = _proj(hs, wv, out32=True, outbf=False).reshape(db, dt, nq)
            o = _attn_b_sample(q, cache_b_k[i].reshape(db, past_len, nq), cache_b_v[i].reshape(db, past_len, nq),
                               kn, vn, rel_bias_table, lam_rows, sub_gain_b[i], lam_init, past_len)
            xs = _out_res(o.reshape(1, ns, d), wo, xs, ms[2])
            st_s.append((kn.reshape(db, dt, B_HEADS, 2, HEAD_DIM), vn.reshape(db, dt, B_HEADS, B_VDIM)))
        else:
            wq = w_in_c[i][:, :d].astype(BF16)
            wk = w_in_c[i][:, d:2 * d].astype(BF16)
            wv = w_in_c[i][:, 2 * d:].astype(BF16)
            wo = w_out_c[i].astype(BF16)
            q = _proj(hp, wq, None, QK_SCALE)
            k32, kbf = _proj(hp, wk, out32=True)
            v32, vbf = _proj(hp, wv, out32=True)
            o = _attn_c_prompt(q, kbf, vbf)
            xp = _out_res(o, wo, xp, mp[2])
            st_p.append((k32.reshape(nb, seq, C_HEADS, HEAD_DIM), v32.reshape(nb, seq, C_HEADS, HEAD_DIM)))
            q = _proj(hs, wq, None, QK_SCALE).reshape(db, dt, d)
            kn = _proj(hs, wk, out32=True, outbf=False).reshape(db, dt, d)
            vn = _proj(hs, wv, out32=True, outbf=False).reshape(db, dt, d)
            o = _attn_c_sample(q, cache_c_k[i].reshape(db, past_len, d), cache_c_v[i].reshape(db, past_len, d),
                               kn, vn)
            xs = _out_res(o.reshape(1, ns, d), wo, xs, ms[2])
            st_s.append((kn.reshape(db, dt, C_HEADS, HEAD_DIM), vn.reshape(db, dt, C_HEADS, HEAD_DIM)))

        w_cat, b_row = _router_weights(w_group[l], b_group[l], w_router[l], b_router[l])
        moe_w = (w_cat, b_row,
                 jnp.concatenate([w_gate[l], w_up[l]], axis=-1).astype(BF16),
                 w_down[l].astype(BF16))
        xp = _moe(xp, norm_ffn[l], mp[4], mp[3], mp[5], moe_w)
        xs = _moe(xs, norm_ffn[l], ms[4], ms[3], ms[5], moe_w)

    def collect(states, kind, j):
        parts = [states[l][j] for l in range(DEPTH) if l % N_MIXERS == kind]
        return parts[0][None] if len(parts) == 1 else jnp.stack(parts)

    return (xp, xs.reshape(db, dt, d),
            collect(st_p, 0, 0), collect(st_p, 0, 1),
            collect(st_p, 1, 0), collect(st_p, 1, 1),
            collect(st_p, 2, 0), collect(st_p, 2, 1),
            collect(st_s, 0, 0), collect(st_s, 0, 1),
            collect(st_s, 1, 0), collect(st_s, 1, 1),
            collect(st_s, 2, 0), collect(st_s, 2, 1))
```

```python
import functools
import math

import numpy as np
import jax
import jax.numpy as jnp
from jax import lax
from jax.experimental import pallas as pl
from jax.experimental.pallas import tpu as pltpu

F32 = jnp.float32
BF16 = jnp.bfloat16

D_MODEL = 2048
DEPTH = 4
CHUNK = 64
HEAD_DIM = 128
N_MIXERS = 3
A_HEADS = 16
A_KV_HEADS = 4
A_GROUP = A_HEADS // A_KV_HEADS
WINDOW_CHUNKS = 2
BAND = (WINDOW_CHUNKS + 1) * CHUNK
B_HEADS = 8
B_VDIM = 2 * HEAD_DIM
C_HEADS = 16
N_BUCKETS = 32
MAX_DISTANCE = 128
N_GROUPS = 4
EXPERTS_PER_GROUP = 4
N_EXPERTS = N_GROUPS * EXPERTS_PER_GROUP
D_EXPERT = D_MODEL // 4
EPS = 1e-6
QK_SCALE = HEAD_DIM ** -0.5
LOG2E = math.log2(math.e)
ROW_CHUNK = 32

LANES = 128
MASKED = -1e30
VMEM_LIMIT = 56 * 1024 * 1024

NT_DIMS = (((1,), (1,)), ((), ()))


def _cparams(*sem):
    return pltpu.CompilerParams(dimension_semantics=sem, vmem_limit_bytes=VMEM_LIMIT)


def _lambda_init(layer):
    return 0.8 - 0.6 * math.exp(-0.3 * layer)


def _t5_bucket_np(rel):
    half = N_BUCKETS // 2
    max_exact = half // 2
    n = np.abs(rel)
    nf = np.maximum(n, 1).astype(np.float32)
    large = max_exact + (np.log(nf / np.float32(max_exact)) / np.float32(math.log(MAX_DISTANCE / max_exact))
                         * np.float32(half - max_exact)).astype(np.int32)
    large = np.minimum(large, half - 1)
    return np.where(rel > 0, half, 0) + np.where(n < max_exact, n, large)


def _bias_tile(table, q_pos, k_pos, valid):
    b = _t5_bucket_np(k_pos[None, :] - q_pos[:, None])
    t = jnp.moveaxis(table.astype(F32)[jnp.asarray(b)], -1, 0)
    return jnp.where(jnp.asarray(valid)[None], t, MASKED)


def _mod_spec(mod, tm, tn, imap_rows, imap_one):
    if mod.shape[1] == 1:
        return pl.BlockSpec((None, 1, tn), imap_one)
    return pl.BlockSpec((None, tm, tn), imap_rows)


def _ada_kernel(c_ref, w_ref, b_ref, o_ref):
    c = c_ref[...]
    a = (c / (1.0 + jnp.exp(-c))).astype(BF16)
    o_ref[...] = jnp.dot(a, w_ref[...].astype(BF16), preferred_element_type=F32) + b_ref[...]


def _ada_mod(c_all, w_ada, b_ada):
    rows = c_all.shape[0]
    depth, d, n = w_ada.shape
    tn = 1024
    return pl.pallas_call(
        _ada_kernel,
        grid=(depth, n // tn),
        in_specs=[pl.BlockSpec((rows, d), lambda l, j: (0, 0)),
                  pl.BlockSpec((None, d, tn), lambda l, j: (l, 0, j)),
                  pl.BlockSpec((None, 1, tn), lambda l, j: (l, 0, j))],
        out_specs=pl.BlockSpec((None, rows, tn), lambda l, j: (l, 0, j)),
        out_shape=jax.ShapeDtypeStruct((depth, rows, n), F32),
        compiler_params=_cparams("parallel", "parallel"),
        name="ada_mod",
    )(c_all, w_ada, b_ada.reshape(depth, 1, n))


def _norm_mod_f32(x, g, sc, sh):
    y = x * lax.rsqrt(jnp.mean(x * x, axis=-1, keepdims=True) + EPS) * g
    return y * (1.0 + sc) + sh


def _norm_mod_kernel(x_ref, g_ref, sc_ref, sh_ref, o_ref):
    o_ref[...] = _norm_mod_f32(x_ref[...], g_ref[...], sc_ref[...], sh_ref[...]).astype(o_ref.dtype)


def _norm_mod(x, gain, sc, sh):
    b, s, d = x.shape
    tm = min(512, s)
    rows = lambda bb, i: (bb, i, 0)
    one = lambda bb, i: (bb, 0, 0)
    return pl.pallas_call(
        _norm_mod_kernel,
        grid=(b, s // tm),
        in_specs=[pl.BlockSpec((None, tm, d), rows),
                  pl.BlockSpec((1, d), lambda bb, i: (0, 0)),
                  _mod_spec(sc, tm, d, rows, one),
                  _mod_spec(sh, tm, d, rows, one)],
        out_specs=pl.BlockSpec((None, tm, d), rows),
        out_shape=jax.ShapeDtypeStruct((b, s, d), BF16),
        compiler_params=_cparams("parallel", "parallel"),
        name="norm_mod",
    )(x, gain.reshape(1, d), sc, sh)


def _proj_kernel(*refs, has_gain, scale, out32, outbf):
    x_ref, w_ref = refs[0], refs[1]
    pos = 2
    g_ref = None
    if has_gain:
        g_ref = refs[pos]
        pos += 1
    o32_ref = obf_ref = None
    if out32:
        o32_ref = refs[pos]
        pos += 1
    if outbf:
        obf_ref = refs[pos]
    acc = jnp.dot(x_ref[...], w_ref[...], preferred_element_type=F32)
    tn = acc.shape[1]
    for c in range(tn // HEAD_DIM):
        sl = slice(c * HEAD_DIM, (c + 1) * HEAD_DIM)
        t = acc[:, sl]
        if has_gain:
            t = t * lax.rsqrt(jnp.mean(t * t, axis=-1, keepdims=True) + EPS) * g_ref[...]
        if out32:
            o32_ref[:, sl] = t
        if outbf:
            obf_ref[:, sl] = (t * scale).astype(BF16) if scale != 1.0 else t.astype(BF16)


def _proj(x, w, gain=None, scale=1.0, out32=False, outbf=True):
    b, s, k = x.shape
    n = w.shape[1]
    tm = min(1024, s)
    tn = min(1024, n)
    in_specs = [pl.BlockSpec((None, tm, k), lambda j, bb, i: (bb, i, 0)),
                pl.BlockSpec((k, tn), lambda j, bb, i: (0, j))]
    args = [x, w]
    if gain is not None:
        in_specs.append(pl.BlockSpec((1, HEAD_DIM), lambda j, bb, i: (0, 0)))
        args.append(gain.reshape(1, HEAD_DIM).astype(F32))
    out_specs, out_shape = [], []
    for flag, dt in ((out32, F32), (outbf, BF16)):
        if flag:
            out_specs.append(pl.BlockSpec((None, tm, tn), lambda j, bb, i: (bb, i, j)))
            out_shape.append(jax.ShapeDtypeStruct((b, s, n), dt))
    outs = pl.pallas_call(
        functools.partial(_proj_kernel, has_gain=gain is not None, scale=scale, out32=out32, outbf=outbf),
        grid=(n // tn, b, s // tm),
        in_specs=in_specs, out_specs=out_specs, out_shape=out_shape,
        compiler_params=_cparams("parallel", "parallel", "parallel"),
        name="proj",
    )(*args)
    return outs if len(outs) > 1 else outs[0]


def _out_res_kernel(o_ref, w_ref, x_ref, g_ref, y_ref):
    acc = jnp.dot(o_ref[...], w_ref[...], preferred_element_type=F32)
    y_ref[...] = x_ref[...] + g_ref[...] * acc


def _out_res(o, w, x, gate):
    b, s, k = o.shape
    n = w.shape[1]
    tm = min(1024, s)
    tn = min(1024, n)
    rows = lambda j, bb, i: (bb, i, j)
    one = lambda j, bb, i: (bb, 0, j)
    return pl.pallas_call(
        _out_res_kernel,
        grid=(n // tn, b, s // tm),
        in_specs=[pl.BlockSpec((None, tm, k), lambda j, bb, i: (bb, i, 0)),
                  pl.BlockSpec((k, tn), lambda j, bb, i: (0, j)),
                  pl.BlockSpec((None, tm, tn), rows),
                  _mod_spec(gate, tm, tn, rows, one)],
        out_specs=pl.BlockSpec((None, tm, tn), rows),
        out_shape=jax.ShapeDtypeStruct((b, s, n), F32),
        compiler_params=_cparams("parallel", "parallel", "parallel"),
        name="out_res",
    )(o, w, x, gate)


def _sink_attend(q, k_parts, v_parts, bias_parts, sink):
    s_parts = [lax.dot_general(q, kp, NT_DIMS, preferred_element_type=F32) + bp
               for kp, bp in zip(k_parts, bias_parts)]
    m = sink
    for sp in s_parts:
        m = jnp.maximum(m, jnp.max(sp, axis=-1, keepdims=True))
    denom = jnp.exp(sink - m)
    e_parts = []
    for sp in s_parts:
        e = jnp.exp(sp - m)
        denom = denom + jnp.sum(e, axis=-1, keepdims=True)
        e_parts.append(e)
    inv = 1.0 / denom
    o = None
    for e, vp in zip(e_parts, v_parts):
        t = jnp.dot((e * inv).astype(BF16), vp, preferred_element_type=F32)
        o = t if o is None else o + t
    return o


def _attn_a_prompt_kernel(q_ref, kp_ref, k_ref, vp_ref, v_ref, bias_ref, bias0_ref, bias1_ref, sink_ref, o_ref,
                          kc_sc, vc_sc, s_sc, p_sc, *, tq):
    i = pl.program_id(2)
    prev = WINDOW_CHUNKS * CHUNK
    kc_sc[:prev] = kp_ref[...]
    kc_sc[prev:] = k_ref[...]
    vc_sc[:prev] = vp_ref[...]
    vc_sc[prev:] = v_ref[...]
    first = i == 0
    n_chunks = tq // CHUNK

    def scores(c):
        rs = slice(c * CHUNK, (c + 1) * CHUNK)
        qc = jnp.concatenate([q_ref[rs, g * HEAD_DIM:(g + 1) * HEAD_DIM] for g in range(A_GROUP)], axis=0)
        s_sc[c] = lax.dot_general(qc, kc_sc[c * CHUNK:c * CHUNK + BAND], NT_DIMS, preferred_element_type=F32)

    def softmax(c):
        bias = bias_ref[...]
        if c == 0:
            bias = jnp.where(first, bias0_ref[...], bias)
        elif c == 1:
            bias = jnp.where(first, bias1_ref[...], bias)
        s = s_sc[c] + bias
        sink = sink_ref[...]
        m = jnp.maximum(sink, jnp.max(s, axis=-1, keepdims=True))
        e = jnp.exp(s - m)
        inv = 1.0 / (jnp.exp(sink - m) + jnp.sum(e, axis=-1, keepdims=True))
        p_sc[c] = (e * inv).astype(BF16)

    def values(c):
        rs = slice(c * CHUNK, (c + 1) * CHUNK)
        o = jnp.dot(p_sc[c], vc_sc[c * CHUNK:c * CHUNK + BAND], preferred_element_type=F32)
        for g in range(A_GROUP):
            o_ref[rs, g * HEAD_DIM:(g + 1) * HEAD_DIM] = o[g * CHUNK:(g + 1) * CHUNK].astype(o_ref.dtype)

    stages = (scores, softmax, values)
    for tick in range(n_chunks + len(stages) - 1):
        for si, stage in enumerate(stages):
            if 0 <= tick - si < n_chunks:
                stage(tick - si)


def _attn_a_prompt(q, k, v, table, sinks):
    b, s, _ = q.shape
    tq = min(512, s)
    prev = WINDOW_CHUNKS * CHUNK
    r = tq // prev
    k_loc = np.arange(BAND)
    q_loc = prev + np.arange(CHUNK)

    def tiles(first_key):
        valid = np.broadcast_to((k_loc >= first_key)[None, :], (CHUNK, BAND))
        t = _bias_tile(table, q_loc, k_loc, valid)
        return t.reshape(A_KV_HEADS, A_GROUP * CHUNK, BAND)

    bias, bias0, bias1 = tiles(0), tiles(prev), tiles(CHUNK)
    sink_col = jnp.repeat(sinks.astype(F32).reshape(A_KV_HEADS, A_GROUP), CHUNK, axis=1)[..., None]
    qmap = lambda bb, h, i: (bb, i, h)
    pmap = lambda bb, h, i: (bb, jnp.maximum(i * r - 1, 0), h)
    bmap = lambda bb, h, i: (h, 0, 0)
    bspec = pl.BlockSpec((None, A_GROUP * CHUNK, BAND), bmap)
    return pl.pallas_call(
        functools.partial(_attn_a_prompt_kernel, tq=tq),
        grid=(b, A_KV_HEADS, s // tq),
        in_specs=[pl.BlockSpec((None, tq, A_GROUP * HEAD_DIM), qmap),
                  pl.BlockSpec((None, prev, HEAD_DIM), pmap),
                  pl.BlockSpec((None, tq, HEAD_DIM), qmap),
                  pl.BlockSpec((None, prev, HEAD_DIM), pmap),
                  pl.BlockSpec((None, tq, HEAD_DIM), qmap),
                  bspec, bspec, bspec,
                  pl.BlockSpec((None, A_GROUP * CHUNK, 1), bmap)],
        out_specs=pl.BlockSpec((None, tq, A_GROUP * HEAD_DIM), qmap),
        out_shape=jax.ShapeDtypeStruct((b, s, D_MODEL), BF16),
        scratch_shapes=[pltpu.VMEM((prev + tq, HEAD_DIM), BF16), pltpu.VMEM((prev + tq, HEAD_DIM), BF16),
                        pltpu.VMEM((tq // CHUNK, A_GROUP * CHUNK, BAND), F32),
                        pltpu.VMEM((tq // CHUNK, A_GROUP * CHUNK, BAND), BF16)],
        compiler_params=_cparams("parallel", "parallel", "parallel"),
        name="attn_a_prompt",
    )(q, k, k, v, v, bias, bias0, bias1, sink_col)


def _attn_a_sample_kernel(q_ref, kc_ref, kn_ref, vc_ref, vn_ref, bc_ref, bn_ref, sink_ref, o_ref, *, t):
    qs = jnp.concatenate([q_ref[:, g * HEAD_DIM:(g + 1) * HEAD_DIM] for g in range(A_GROUP)], axis=0)
    o = _sink_attend(qs,
                     [kc_ref[...].astype(BF16), kn_ref[...].astype(BF16)],
                     [vc_ref[...].astype(BF16), vn_ref[...].astype(BF16)],
                     [bc_ref[...], bn_ref[...]], sink_ref[...])
    for g in range(A_GROUP):
        o_ref[:, g * HEAD_DIM:(g + 1) * HEAD_DIM] = o[g * t:(g + 1) * t].astype(o_ref.dtype)


def _attn_a_sample(q, ck, cv, kn, vn, table, sinks, past_len):
    b, t, _ = q.shape
    cache = ck.shape[1]
    q_pos = past_len + np.arange(t)
    k_pos = np.concatenate([past_len - cache + np.arange(cache), q_pos])
    qc, kc = q_pos[:, None] // CHUNK, k_pos[None, :] // CHUNK
    valid = (kc <= qc) & (kc >= qc - WINDOW_CHUNKS)
    bias = _bias_tile(table, q_pos, k_pos, valid).reshape(A_KV_HEADS, A_GROUP * t, cache + t)
    sink_col = jnp.repeat(sinks.astype(F32).reshape(A_KV_HEADS, A_GROUP), t, axis=1)[..., None]
    hmap = lambda bb, h: (bb, 0, h)
    bmap = lambda bb, h: (h, 0, 0)
    return pl.pallas_call(
        functools.partial(_attn_a_sample_kernel, t=t),
        grid=(b, A_KV_HEADS),
        in_specs=[pl.BlockSpec((None, t, A_GROUP * HEAD_DIM), hmap),
                  pl.BlockSpec((None, cache, HEAD_DIM), hmap),
                  pl.BlockSpec((None, t, HEAD_DIM), hmap),
                  pl.BlockSpec((None, cache, HEAD_DIM), hmap),
                  pl.BlockSpec((None, t, HEAD_DIM), hmap),
                  pl.BlockSpec((None, A_GROUP * t, cache), bmap),
                  pl.BlockSpec((None, A_GROUP * t, t), bmap),
                  pl.BlockSpec((None, A_GROUP * t, 1), bmap)],
        out_specs=pl.BlockSpec((None, t, A_GROUP * HEAD_DIM), hmap),
        out_shape=jax.ShapeDtypeStruct((b, t, D_MODEL), BF16),
        compiler_params=_cparams("parallel", "parallel"),
        name="attn_a_sample",
    )(q, ck, kn, cv, vn, bias[..., :cache], bias[..., cache:], sink_col)


def _diff_lambda(lam_ref, lam_init):
    lp = lam_ref[...]
    a = jnp.sum(lp[0:1] * lp[1:2], axis=-1, keepdims=True)
    c = jnp.sum(lp[2:3] * lp[3:4], axis=-1, keepdims=True)
    return jnp.exp(a) - jnp.exp(c) + lam_init


def _finish_b(o0, o1, lam, gsub, lam_init):
    o = o0 - lam * o1
    o = o * lax.rsqrt(jnp.mean(o * o, axis=-1, keepdims=True) + EPS) * gsub
    return o * (1.0 - lam_init)


def _attn_b_prompt_kernel(q_ref, k_ref, v_ref, bias_ref, lam_ref, gsub_ref, o_ref,
                          s0_sc, s1_sc, p0_sc, p1_sc, al_sc, m_sc, l_sc, acc_sc, *, t, lam_init):
    i = pl.program_id(2)
    s_scs = (s0_sc, s1_sc)
    p_scs = (p0_sc, p1_sc)
    m_sc[...] = jnp.full(m_sc.shape, MASKED, F32)
    l_sc[...] = jnp.zeros(l_sc.shape, F32)
    acc_sc[...] = jnp.zeros(acc_sc.shape, F32)

    def keys(j):
        return pl.ds(pl.multiple_of(j * t, t), t)

    def scores(j, mp):
        sl = slice(mp * HEAD_DIM, (mp + 1) * HEAD_DIM)
        s_scs[mp][...] = jnp.dot(q_ref[:, sl], k_ref[sl, keys(j)], preferred_element_type=F32)

    def softmax(j, mp):
        kind = jnp.minimum(i - j, 2)
        s = s_scs[mp][...] + bias_ref[mp, kind]
        m_old = m_sc[mp]
        m_new = jnp.maximum(m_old, jnp.max(s, axis=-1, keepdims=True))
        alpha = jnp.exp2(m_old - m_new)
        p = jnp.exp2(s - m_new)
        l_sc[mp] = alpha * l_sc[mp] + jnp.sum(p, axis=-1, keepdims=True)
        m_sc[mp] = m_new
        al_sc[mp] = alpha
        p_scs[mp][...] = p.astype(BF16)

    def weigh(j, mp):
        acc_sc[mp] = al_sc[mp] * acc_sc[mp] + jnp.dot(p_scs[mp][...], v_ref[keys(j), :],
                                                      preferred_element_type=F32)

    scores(0, 0)
    scores(0, 1)
    softmax(0, 0)

    def body(j, carry):
        scores(j, 0)
        softmax(j - 1, 1)
        weigh(j - 1, 0)
        scores(j, 1)
        softmax(j, 0)
        weigh(j - 1, 1)
        return carry

    lax.fori_loop(1, i + 1, body, 0)
    softmax(i, 1)
    weigh(i, 0)
    weigh(i, 1)
    lam = _diff_lambda(lam_ref, lam_init)
    o = _finish_b(acc_sc[0] / l_sc[0], acc_sc[1] / l_sc[1], lam, gsub_ref[...], lam_init)
    o_ref[...] = o.astype(o_ref.dtype)


def _toeplitz(f, t):
    h = f.shape[0]
    g = jnp.pad(f, ((0, 0), (0, 1)))
    flat = jnp.tile(g, (1, t))[:, :t * (2 * t - 1)]
    return flat.reshape(h, t, 2 * t - 1)[:, :, t - 1:]


def _attn_b_prompt(q, k, v, table, lam_rows, gsub, lam_init):
    b, s, _ = q.shape
    t = min(512, s)
    loc = np.arange(t)
    chunk_ok = jnp.asarray((loc[None, :] // CHUNK) <= (loc[:, None] // CHUNK))
    tab = table.astype(F32) * LOG2E
    rel = np.arange(-(t - 1), t)
    diag = _toeplitz(tab[jnp.asarray(_t5_bucket_np(rel))].T, t)
    diag = jnp.where(chunk_ok[None], diag, MASKED)
    sub = _toeplitz(tab[jnp.asarray(_t5_bucket_np(rel - t))].T, t)
    far_rel = -(t + 1 + np.arange(max(s - t, 1)))
    far_bucket = _t5_bucket_np(far_rel)
    assert np.all(far_bucket == far_bucket[0])
    far = jnp.broadcast_to(tab[int(far_bucket[0])][:, None, None], (2 * B_HEADS, t, t))
    bias = jnp.stack([diag, sub, far], axis=1)
    qmap = lambda bb, h, i: (bb, i, h)
    kmap = lambda bb, h, i: (bb, 0, h)
    return pl.pallas_call(
        functools.partial(_attn_b_prompt_kernel, t=t, lam_init=lam_init),
        grid=(b, B_HEADS, s // t),
        in_specs=[pl.BlockSpec((None, t, B_VDIM), qmap),
                  pl.BlockSpec((None, B_VDIM, s), lambda bb, h, i: (bb, h, 0)),
                  pl.BlockSpec((None, s, B_VDIM), kmap),
                  pl.BlockSpec((2, 3, t, t), lambda bb, h, i: (h, 0, 0, 0)),
                  pl.BlockSpec((4, HEAD_DIM), lambda bb, h, i: (0, 0)),
                  pl.BlockSpec((1, B_VDIM), lambda bb, h, i: (0, 0))],
        out_specs=pl.BlockSpec((None, t, B_VDIM), qmap),
        out_shape=jax.ShapeDtypeStruct((b, s, D_MODEL), BF16),
        scratch_shapes=[pltpu.VMEM((t, t), F32), pltpu.VMEM((t, t), F32),
                        pltpu.VMEM((t, t), BF16), pltpu.VMEM((t, t), BF16),
                        pltpu.VMEM((2, t, 1), F32), pltpu.VMEM((2, t, 1), F32), pltpu.VMEM((2, t, 1), F32),
                        pltpu.VMEM((2, t, B_VDIM), F32)],
        compiler_params=_cparams("parallel", "parallel", "parallel"),
        name="attn_b_prompt",
    )(q, jnp.swapaxes(k, 1, 2), v, bias, lam_rows, gsub.reshape(1, B_VDIM).astype(F32))


def _attn_b_sample_kernel(q_ref, kc_ref, kn_ref, vc_ref, vn_ref, bc_ref, bn_ref, lam_ref, gsub_ref, o_ref,
                          *, lam_init):
    kc = kc_ref[...].astype(BF16)
    kn = kn_ref[...].astype(BF16)
    vc = vc_ref[...].astype(BF16)
    vn = vn_ref[...].astype(BF16)
    outs = []
    for mp in range(2):
        sl = slice(mp * HEAD_DIM, (mp + 1) * HEAD_DIM)
        qm = q_ref[:, sl]
        sc = lax.dot_general(qm, kc[:, sl], NT_DIMS, preferred_element_type=F32) + bc_ref[mp]
        sn = lax.dot_general(qm, kn[:, sl], NT_DIMS, preferred_element_type=F32) + bn_ref[mp]
        m = jnp.maximum(jnp.max(sc, axis=-1, keepdims=True), jnp.max(sn, axis=-1, keepdims=True))
        pc = jnp.exp(sc - m)
        pn = jnp.exp(sn - m)
        inv = 1.0 / (jnp.sum(pc, axis=-1, keepdims=True) + jnp.sum(pn, axis=-1, keepdims=True))
        outs.append(jnp.dot((pc * inv).astype(BF16), vc, preferred_element_type=F32)
                    + jnp.dot((pn * inv).astype(BF16), vn, preferred_element_type=F32))
    lam = _diff_lambda(lam_ref, lam_init)
    o_ref[...] = _finish_b(outs[0], outs[1], lam, gsub_ref[...], lam_init).astype(o_ref.dtype)


def _attn_b_sample(q, ck, cv, kn, vn, table, lam_rows, gsub, lam_init, past_len):
    b, t, _ = q.shape
    past = ck.shape[1]
    q_pos = past_len + np.arange(t)
    k_pos = np.arange(past_len + t)
    valid = (k_pos[None, :] // CHUNK) <= (q_pos[:, None] // CHUNK)
    bias = _bias_tile(table, q_pos, k_pos, valid).reshape(B_HEADS, 2, t, past + t)
    hmap = lambda bb, h: (bb, 0, h)
    bmap = lambda bb, h: (h, 0, 0, 0)
    return pl.pallas_call(
        functools.partial(_attn_b_sample_kernel, lam_init=lam_init),
        grid=(b, B_HEADS),
        in_specs=[pl.BlockSpec((None, t, B_VDIM), hmap),
                  pl.BlockSpec((None, past, B_VDIM), hmap),
                  pl.BlockSpec((None, t, B_VDIM), hmap),
                  pl.BlockSpec((None, past, B_VDIM), hmap),
                  pl.BlockSpec((None, t, B_VDIM), hmap),
                  pl.BlockSpec((None, 2, t, past), bmap),
                  pl.BlockSpec((None, 2, t, t), bmap),
                  pl.BlockSpec((4, HEAD_DIM), lambda bb, h: (0, 0)),
                  pl.BlockSpec((1, B_VDIM), lambda bb, h: (0, 0))],
        out_specs=pl.BlockSpec((None, t, B_VDIM), hmap),
        out_shape=jax.ShapeDtypeStruct((b, t, D_MODEL), BF16),
        compiler_params=_cparams("parallel", "parallel"),
        name="attn_b_sample",
    )(q, ck, kn, cv, vn, bias[..., :past], bias[..., past:], lam_rows, gsub.reshape(1, B_VDIM).astype(F32))


def _neg_abs(x):
    bits = lax.bitcast_convert_type(x, jnp.uint32) | jnp.uint32(0x80000000)
    return lax.bitcast_convert_type(bits, F32)


def _minus_later_ones(n):
    idx = np.arange(n)
    return jnp.asarray(-(idx[:, None] > idx[None, :]).astype(np.float32), dtype=BF16)


def _sb_block(q, kb, vb, u, run, mask):
    z = lax.dot_general(q, kb, NT_DIMS, preferred_element_type=F32)
    sp = jnp.maximum(z, 0.0) + jnp.log(1.0 + jnp.exp(-jnp.abs(z)))
    if mask is not None:
        sp = jnp.where(mask, sp, 0.0)
    after = jnp.dot(sp.astype(BF16), u, preferred_element_type=F32) + run
    a = jnp.exp((z - sp) + after)
    if mask is not None:
        a = jnp.where(mask, a, 0.0)
    o = jnp.dot(a.astype(BF16), vb, preferred_element_type=F32)
    return o, run - jnp.sum(sp, axis=-1, keepdims=True)


C_HEADS_PER_STEP = 4


def _attn_c_prompt_kernel(q_ref, k_ref, v_ref, u_ref, o_ref, z_sc, spb_sc, aft_sc, a_sc, off_sc, run_sc, acc_sc,
                          *, t):
    i = pl.program_id(2)
    hp = C_HEADS_PER_STEP
    heads = [slice(h * HEAD_DIM, (h + 1) * HEAD_DIM) for h in range(hp)]
    run_sc[...] = jnp.zeros(run_sc.shape, F32)
    acc_sc[...] = jnp.zeros(acc_sc.shape, F32)

    def block(start, masked):
        keys = pl.ds(start, t)
        if masked:
            visible = lax.broadcasted_iota(jnp.int32, (t, t), 1) < lax.broadcasted_iota(jnp.int32, (t, t), 0)

        def scores(h):
            z_sc[h] = lax.dot_general(q_ref[:, heads[h]], k_ref[keys, heads[h]], NT_DIMS,
                                      preferred_element_type=F32)

        def softplus(h):
            z = z_sc[h]
            sp = jnp.maximum(z, 0.0) + jnp.log(1.0 + jnp.exp(_neg_abs(z)))
            if masked:
                sp = jnp.where(visible, sp, 0.0)
            z_sc[h] = z - sp
            spb_sc[h] = sp.astype(BF16)
            run = run_sc[h]
            off_sc[h] = jnp.exp(run)
            run_sc[h] = run - jnp.sum(sp, axis=-1, keepdims=True)

        def later_sums(h):
            aft_sc[h] = jnp.dot(spb_sc[h], u_ref[...], preferred_element_type=F32)

        def weights(h):
            a = jnp.exp(z_sc[h] + aft_sc[h])
            if masked:
                a = jnp.where(visible, a, 0.0)
            a_sc[h] = a.astype(BF16)

        def values(h):
            acc_sc[h] += off_sc[h] * jnp.dot(a_sc[h], v_ref[keys, heads[h]], preferred_element_type=F32)

        stages = (scores, softplus, later_sums, weights, values)
        for tick in range(hp + len(stages) - 1):
            for si, stage in enumerate(stages):
                if 0 <= tick - si < hp:
                    stage(tick - si)

    block(pl.multiple_of(i * t, t), True)

    def body(n, carry):
        block(pl.multiple_of((i - 1 - n) * t, t), False)
        return carry

    lax.fori_loop(0, i, body, 0)
    for h in range(hp):
        o_ref[:, heads[h]] = acc_sc[h].astype(o_ref.dtype)


def _attn_c_prompt(q, k, v):
    b, s, _ = q.shape
    t = min(256, s)
    hp = C_HEADS_PER_STEP
    qmap = lambda bb, h, i: (bb, i, h)
    kmap = lambda bb, h, i: (bb, 0, h)
    return pl.pallas_call(
        functools.partial(_attn_c_prompt_kernel, t=t),
        grid=(b, C_HEADS // hp, s // t),
        in_specs=[pl.BlockSpec((None, t, hp * HEAD_DIM), qmap),
                  pl.BlockSpec((None, s, hp * HEAD_DIM), kmap),
                  pl.BlockSpec((None, s, hp * HEAD_DIM), kmap),
                  pl.BlockSpec((t, t), lambda bb, h, i: (0, 0))],
        out_specs=pl.BlockSpec((None, t, hp * HEAD_DIM), qmap),
        out_shape=jax.ShapeDtypeStruct((b, s, D_MODEL), BF16),
        scratch_shapes=[pltpu.VMEM((hp, t, t), F32), pltpu.VMEM((hp, t, t), BF16), pltpu.VMEM((hp, t, t), F32),
                        pltpu.VMEM((hp, t, t), BF16), pltpu.VMEM((hp, t, 1), F32),
                        pltpu.VMEM((hp, t, 1), F32), pltpu.VMEM((hp, t, HEAD_DIM), F32)],
        compiler_params=_cparams("parallel", "parallel", "parallel"),
        name="attn_c_prompt",
    )(q, k, v, _minus_later_ones(t))


def _attn_c_sample_kernel(q_ref, kc_ref, kn_ref, vc_ref, vn_ref, u_ref, o_ref, *, t, past, tk):
    q = q_ref[...]
    u = u_ref[...]
    row = lax.broadcasted_iota(jnp.int32, (t, t), 0)
    col = lax.broadcasted_iota(jnp.int32, (t, t), 1)
    acc, run = _sb_block(q, kn_ref[...].astype(BF16), vn_ref[...].astype(BF16), u[:t, :t],
                         jnp.zeros((t, 1), F32), col < row)
    for j in reversed(range(past // tk)):
        rs = slice(j * tk, (j + 1) * tk)
        o, run = _sb_block(q, kc_ref[rs, :].astype(BF16), vc_ref[rs, :].astype(BF16), u, run, None)
        acc = acc + o
    o_ref[...] = acc.astype(o_ref.dtype)


def _attn_c_sample(q, ck, cv, kn, vn):
    b, t, _ = q.shape
    past = ck.shape[1]
    tk = min(256, past)
    hmap = lambda bb, h: (bb, 0, h)
    return pl.pallas_call(
        functools.partial(_attn_c_sample_kernel, t=t, past=past, tk=tk),
        grid=(b, C_HEADS),
        in_specs=[pl.BlockSpec((None, t, HEAD_DIM), hmap),
                  pl.BlockSpec((None, past, HEAD_DIM), hmap),
                  pl.BlockSpec((None, t, HEAD_DIM), hmap),
                  pl.BlockSpec((None, past, HEAD_DIM), hmap),
                  pl.BlockSpec((None, t, HEAD_DIM), hmap),
                  pl.BlockSpec((tk, tk), lambda bb, h: (0, 0))],
        out_specs=pl.BlockSpec((None, t, HEAD_DIM), hmap),
        out_shape=jax.ShapeDtypeStruct((b, t, D_MODEL), BF16),
        compiler_params=_cparams("parallel", "parallel"),
        name="attn_c_sample",
    )(q, ck, kn, cv, vn, _minus_later_ones(tk))


ROUTE_LANES = LANES
EXPERT_LANE0 = N_GROUPS
ROW_LANES = LANES
ROW_TILE = D_MODEL // ROW_LANES


def _store_row_tiled(ref, val):
    rows = val.shape[0]
    for s in range(ROW_TILE):
        ref[pl.ds(s, rows, stride=ROW_TILE), :] = val[:, s * ROW_LANES:(s + 1) * ROW_LANES]


def _load_row_tiled(ref, rows, s, base=0):
    return ref[pl.ds(base * ROW_TILE + s, rows, stride=ROW_TILE), :]


def _router_kernel(x_ref, g_ref, sc_ref, sh_ref, w_ref, b_ref, h_ref, r_ref):
    h = _norm_mod_f32(x_ref[...], g_ref[...], sc_ref[...], sh_ref[...])
    _store_row_tiled(h_ref, h)
    h_hi = h.astype(BF16)
    h_lo = (h - h_hi.astype(F32)).astype(BF16)
    w = w_ref[...]
    both = jnp.dot(h_hi, w, preferred_element_type=F32)
    logits = (both[:, :ROUTE_LANES] + both[:, ROUTE_LANES:]
              + jnp.dot(h_lo, w[:, :ROUTE_LANES], preferred_element_type=F32) + b_ref[...])
    lane = lax.broadcasted_iota(jnp.int32, logits.shape, 1)
    lanef = lane.astype(F32)

    def first_lane_of_max(v, vmax):
        return jnp.min(jnp.where(v == vmax, lanef, float(ROUTE_LANES)), axis=-1, keepdims=True)

    lg = jnp.where(lane < N_GROUPS, logits, MASKED)
    mg = jnp.max(lg, axis=-1, keepdims=True)
    gate = 1.0 / jnp.sum(jnp.exp(lg - mg), axis=-1, keepdims=True)
    gi = first_lane_of_max(lg, mg)
    lo = EXPERT_LANE0 + EXPERTS_PER_GROUP * gi
    le = jnp.where((lanef >= lo) & (lanef < lo + EXPERTS_PER_GROUP), logits, MASKED)
    v1 = jnp.max(le, axis=-1, keepdims=True)
    i1 = first_lane_of_max(le, v1)
    le2 = jnp.where(lanef == i1, MASKED, le)
    v2 = jnp.max(le2, axis=-1, keepdims=True)
    i2 = first_lane_of_max(le2, v2)
    e21 = jnp.exp(v2 - v1)
    w1 = gate / (1.0 + e21)
    w2 = w1 * e21
    r_ref[...] = jnp.where(lane == 0, i1 - EXPERT_LANE0,
                           jnp.where(lane == 1, i2 - EXPERT_LANE0,
                                     jnp.where(lane == 2, w1, jnp.where(lane == 3, w2, 0.0))))


def _router(x, gain, sc, sh, w_cat, b_row):
    b, s, d = x.shape
    tm = min(512, s)
    rows = lambda bb, i: (bb, i, 0)
    one = lambda bb, i: (bb, 0, 0)
    return pl.pallas_call(
        _router_kernel,
        grid=(b, s // tm),
        in_specs=[pl.BlockSpec((None, tm, d), rows),
                  pl.BlockSpec((1, d), lambda bb, i: (0, 0)),
                  _mod_spec(sc, tm, d, rows, one),
                  _mod_spec(sh, tm, d, rows, one),
                  pl.BlockSpec((d, 2 * ROUTE_LANES), lambda bb, i: (0, 0)),
                  pl.BlockSpec((1, ROUTE_LANES), lambda bb, i: (0, 0))],
        out_specs=[pl.BlockSpec((None, tm * ROW_TILE, ROW_LANES), rows),
                   pl.BlockSpec((None, tm, ROUTE_LANES), rows)],
        out_shape=[jax.ShapeDtypeStruct((b, s * ROW_TILE, ROW_LANES), F32),
                   jax.ShapeDtypeStruct((b, s, ROUTE_LANES), F32)],
        compiler_params=_cparams("parallel", "parallel"),
        name="moe_router",
    )(x, gain.reshape(1, d), sc, sh, w_cat, b_row)


def _router_weights(w_group, b_group, w_router, b_router):
    w = jnp.concatenate([w_group, jnp.moveaxis(w_router, 0, 1).reshape(D_MODEL, N_EXPERTS)], axis=1)
    w = jnp.pad(w.astype(F32), ((0, 0), (0, ROUTE_LANES - w.shape[1])))
    hi = w.astype(BF16)
    lo = (w - hi.astype(F32)).astype(BF16)
    bias = jnp.concatenate([b_group, b_router.reshape(N_EXPERTS)]).astype(F32)
    bias = jnp.pad(bias, (0, ROUTE_LANES - bias.shape[0])).reshape(1, ROUTE_LANES)
    return jnp.concatenate([hi, lo], axis=1), bias


def _row_gather(src_hbm, idx_ref, n_rows, dst, sem, inline=False, alternate=False):
    def start(r):
        src = pl.multiple_of(idx_ref[0, r] * ROW_TILE, ROW_TILE)
        static = isinstance(r, int)
        dst_rows = r * ROW_TILE if static else pl.multiple_of(r * ROW_TILE, ROW_TILE)
        pltpu.make_async_copy(src_hbm.at[pl.ds(src, ROW_TILE)], dst.at[pl.ds(dst_rows, ROW_TILE)],
                              sem).start(priority=r % 2 if (static and alternate) else 0)

    if inline:
        for r in range(n_rows):
            start(r)
    else:
        def body(r, carry):
            start(r)
            return carry
        lax.fori_loop(0, n_rows, body, 0, unroll=8)


def _row_gather_wait(src_hbm, n_rows, dst, sem):
    pltpu.make_async_copy(src_hbm.at[pl.ds(0, n_rows * ROW_TILE)], dst, sem).wait()


EXPERT_RING = 3


def _expert_kernel(te_ref, idx_ref, idx_n1_ref, idx_n2_ref, h_hbm, wgu_ref, wd_ref, y_ref, buf, sem, *, tm):
    i = pl.program_id(0)
    n = pl.num_programs(0)
    slot = i % EXPERT_RING
    ahead = (i + 2) % EXPERT_RING

    @pl.when(i == 0)
    def _():
        _row_gather(h_hbm, idx_ref, tm, buf.at[0], sem.at[0])
        _row_gather(h_hbm, idx_n1_ref, tm, buf.at[1], sem.at[1])

    _row_gather_wait(h_hbm, tm, buf.at[slot], sem.at[slot])
    _row_gather(h_hbm, idx_n2_ref, tm, buf.at[ahead], sem.at[ahead], inline=True)
    xb = buf.at[slot]
    x = jnp.concatenate([_load_row_tiled(xb, tm, s).astype(BF16) for s in range(ROW_TILE)], axis=1)
    gu = jnp.dot(x, wgu_ref[...], preferred_element_type=F32)
    gate, up = gu[:, :D_EXPERT], gu[:, D_EXPERT:]
    hid = (gate / (1.0 + jnp.exp(-gate)) * up).astype(BF16)
    _store_row_tiled(y_ref, jnp.dot(hid, wd_ref[...], preferred_element_type=F32))

    @pl.when(i == n - 1)
    def _():
        other = (i + 1) % EXPERT_RING
        _row_gather_wait(h_hbm, tm, buf.at[other], sem.at[other])
        _row_gather_wait(h_hbm, tm, buf.at[ahead], sem.at[ahead])


def _experts(h_rt, tile_expert, src_rows, w_gu, w_down, tm):
    n_tiles = tile_expert.shape[0]
    d = D_MODEL
    grid_spec = pltpu.PrefetchScalarGridSpec(
        num_scalar_prefetch=1,
        grid=(n_tiles,),
        in_specs=[pl.BlockSpec((None, 1, tm), lambda i, te: (i, 0, 0), memory_space=pltpu.SMEM),
                  pl.BlockSpec((None, 1, tm), lambda i, te: (jnp.minimum(i + 1, n_tiles - 1), 0, 0),
                               memory_space=pltpu.SMEM),
                  pl.BlockSpec((None, 1, tm), lambda i, te: (jnp.minimum(i + 2, n_tiles - 1), 0, 0),
                               memory_space=pltpu.SMEM),
                  pl.BlockSpec(memory_space=pl.ANY),
                  pl.BlockSpec((None, d, 2 * D_EXPERT), lambda i, te: (te[i], 0, 0)),
                  pl.BlockSpec((None, D_EXPERT, d), lambda i, te: (te[i], 0, 0))],
        out_specs=pl.BlockSpec((tm * ROW_TILE, ROW_LANES), lambda i, te: (i, 0)),
        scratch_shapes=[pltpu.VMEM((EXPERT_RING, tm * ROW_TILE, ROW_LANES), F32),
                        pltpu.SemaphoreType.DMA((EXPERT_RING,))],
    )
    assert n_tiles >= EXPERT_RING
    idx3 = src_rows.reshape(n_tiles, 1, tm)
    return pl.pallas_call(
        functools.partial(_expert_kernel, tm=tm),
        grid_spec=grid_spec,
        out_shape=jax.ShapeDtypeStruct((n_tiles * tm * ROW_TILE, ROW_LANES), F32),
        compiler_params=_cparams("arbitrary"),
        name="moe_experts",
    )(tile_expert, idx3, idx3, idx3, h_rt, w_gu, w_down)


def _combine_kernel(pos_ref, pos_next_ref, y_hbm, x_ref, g_ref, r_ref, o_ref, buf0, buf1, sem, *, tc):
    i = pl.program_id(0) * pl.num_programs(1) + pl.program_id(1)
    n = pl.num_programs(0) * pl.num_programs(1)

    @pl.when(i == 0)
    def _():
        _row_gather(y_hbm, pos_ref, 2 * tc, buf0, sem.at[0])

    def tile(cur, cur_sem, nxt, nxt_sem):
        _row_gather_wait(y_hbm, 2 * tc, cur, cur_sem)
        _row_gather(y_hbm, pos_next_ref, 2 * tc, nxt, nxt_sem, inline=True, alternate=True)
        w1 = r_ref[:, 2:3]
        w2 = r_ref[:, 3:4]
        for s in range(ROW_TILE):
            sl = slice(s * ROW_LANES, (s + 1) * ROW_LANES)
            y = w1 * _load_row_tiled(cur, tc, s) + w2 * _load_row_tiled(cur, tc, s, base=tc)
            o_ref[:, sl] = x_ref[:, sl] + g_ref[:, sl] * y

        @pl.when(i == n - 1)
        def _():
            _row_gather_wait(y_hbm, 2 * tc, nxt, nxt_sem)

    @pl.when(i % 2 == 0)
    def _():
        tile(buf0, sem.at[0], buf1, sem.at[1])

    @pl.when(i % 2 == 1)
    def _():
        tile(buf1, sem.at[1], buf0, sem.at[0])


def _combine(x, y_rt, pos, gate, route, tc):
    b, s, d = x.shape
    nt = s // tc
    n = b * nt
    pos3 = pos.reshape(n, tc, 2).transpose(0, 2, 1).reshape(n, 1, 2 * tc)
    rows = lambda bb, i: (bb, i, 0)
    one = lambda bb, i: (bb, 0, 0)
    return pl.pallas_call(
        functools.partial(_combine_kernel, tc=tc),
        grid=(b, nt),
        in_specs=[pl.BlockSpec((None, 1, 2 * tc), lambda bb, i: (bb * nt + i, 0, 0), memory_space=pltpu.SMEM),
                  pl.BlockSpec((None, 1, 2 * tc), lambda bb, i: (jnp.minimum(bb * nt + i + 1, n - 1), 0, 0),
                               memory_space=pltpu.SMEM),
                  pl.BlockSpec(memory_space=pl.ANY),
                  pl.BlockSpec((None, tc, d), rows),
                  _mod_spec(gate, tc, d, rows, one),
                  pl.BlockSpec((None, tc, ROUTE_LANES), rows)],
        out_specs=pl.BlockSpec((None, tc, d), rows),
        out_shape=jax.ShapeDtypeStruct((b, s, d), F32),
        scratch_shapes=[pltpu.VMEM((2 * tc * ROW_TILE, ROW_LANES), F32),
                        pltpu.VMEM((2 * tc * ROW_TILE, ROW_LANES), F32), pltpu.SemaphoreType.DMA((2,))],
        compiler_params=_cparams("arbitrary", "arbitrary"),
        name="moe_combine",
    )(pos3, pos3, y_rt, x, gate, route)


def _moe(x, gain, sc, sh, gate, wts):
    w_cat, b_row, w_gu, w_down = wts
    b, s, d = x.shape
    tokens = b * s
    h, route = _router(x, gain, sc, sh, w_cat, b_row)
    ids = route.reshape(tokens, ROUTE_LANES)[:, 0:2].astype(jnp.int32).reshape(-1)
    tm = 256 if tokens >= 4096 else 128
    n_assign = 2 * tokens
    n_tiles = (n_assign + N_EXPERTS * (tm - 1)) // tm + 1
    onehot = (ids[:, None] == jnp.arange(N_EXPERTS)[None, :]).astype(jnp.int32)
    csum = jnp.cumsum(onehot, axis=0)
    counts = csum[-1]
    padded = ((counts + tm - 1) // tm) * tm
    ends = jnp.cumsum(padded)
    pos = jnp.sum(onehot * (csum - 1 + (ends - padded)[None, :]), axis=1)
    tile_expert = jnp.minimum(jnp.searchsorted(ends // tm, jnp.arange(n_tiles), side='right'),
                              N_EXPERTS - 1).astype(jnp.int32)
    src_rows = jnp.zeros((n_tiles * tm,), jnp.int32).at[pos].set(jnp.arange(n_assign, dtype=jnp.int32) // 2)
    y = _experts(h.reshape(tokens * ROW_TILE, ROW_LANES), tile_expert, src_rows, w_gu, w_down, tm)
    return _combine(x, y, pos.reshape(tokens, 2), gate, route, min(256, s))


def _split_mod(mod):
    return [mod[..., j * D_MODEL:(j + 1) * D_MODEL] for j in range(6)]


def kernel(x_prompt, x_sample, c_prompt, c_sample, cache_a_k, cache_a_v, cache_b_k, cache_b_v, cache_c_k, cache_c_v, rel_bias_table, norm_mix, norm_ffn, w_ada, b_ada, w_in_a, q_gain_a, k_gain_a, sinks_a, w_out_a, w_in_b, q_gain_b, k_gain_b, lam_q1, lam_k1, lam_q2, lam_k2, sub_gain_b, w_out_b, w_in_c, w_out_c, w_group, b_group, w_router, b_router, w_gate, w_up, w_down):
    nb, seq, d = x_prompt.shape
    db, dt, _ = x_sample.shape
    past_len = cache_b_k.shape[2]
    a_cache = cache_a_k.shape[2]
    ns = db * dt

    n_c = nb + db
    c_rows = -(-n_c // 16) * 16
    c_all = jnp.pad(jnp.concatenate([c_prompt, c_sample], axis=0).astype(F32), ((0, c_rows - n_c), (0, 0)))
    mods = _ada_mod(c_all, w_ada, b_ada)

    xp = x_prompt
    xs = x_sample.reshape(1, ns, d)
    st_p, st_s = [], []
    for l in range(DEPTH):
        i, kind = l // N_MIXERS, l % N_MIXERS
        mp = _split_mod(mods[l, :nb][:, None, :])
        ms = _split_mod(jnp.repeat(mods[l, nb:n_c], dt, axis=0)[None])
        hp = _norm_mod(xp, norm_mix[l], mp[1], mp[0])
        hs = _norm_mod(xs, norm_mix[l], ms[1], ms[0])

        if kind == 0:
            nq, nk = A_HEADS * HEAD_DIM, A_KV_HEADS * HEAD_DIM
            wq = w_in_a[i][:, :nq].astype(BF16)
            wk = w_in_a[i][:, nq:nq + nk].astype(BF16)
            wv = w_in_a[i][:, nq + nk:].astype(BF16)
            wo = w_out_a[i].astype(BF16)
            q = _proj(hp, wq, q_gain_a[i], QK_SCALE)
            k32, kbf = _proj(hp, wk, k_gain_a[i], out32=True)
            v32, vbf = _proj(hp, wv, out32=True)
            o = _attn_a_prompt(q, kbf, vbf, rel_bias_table, sinks_a[i])
            xp = _out_res(o, wo, xp, mp[2])
            st_p.append((k32[:, -a_cache:].reshape(nb, a_cache, A_KV_HEADS, HEAD_DIM),
                         v32[:, -a_cache:].reshape(nb, a_cache, A_KV_HEADS, HEAD_DIM)))
            q = _proj(hs, wq, q_gain_a[i], QK_SCALE).reshape(db, dt, nq)
            kn = _proj(hs, wk, k_gain_a[i], out32=True, outbf=False).reshape(db, dt, nk)
            vn = _proj(hs, wv, out32=True, outbf=False).reshape(db, dt, nk)
            ck = cache_a_k[i].reshape(db, a_cache, nk)
            cv = cache_a_v[i].reshape(db, a_cache, nk)
            o = _attn_a_sample(q, ck, cv, kn, vn, rel_bias_table, sinks_a[i], past_len)
            xs = _out_res(o.reshape(1, ns, d), wo, xs, ms[2])
            kk = jnp.concatenate([ck, kn], axis=1)[:, -a_cache:]
            vv = jnp.concatenate([cv, vn], axis=1)[:, -a_cache:]
            st_s.append((kk.reshape(db, a_cache, A_KV_HEADS, HEAD_DIM),
                         vv.reshape(db, a_cache, A_KV_HEADS, HEAD_DIM)))
        elif kind == 1:
            nq = B_HEADS * 2 * HEAD_DIM
            lam_init = _lambda_init(l)
            wq = w_in_b[i][:, :nq].astype(BF16)
            wk = w_in_b[i][:, nq:2 * nq].astype(BF16)
            wv = w_in_b[i][:, 2 * nq:].astype(BF16)
            wo = w_out_b[i].astype(BF16)
            lam_rows = jnp.stack([lam_q1[i], lam_k1[i], lam_q2[i], lam_k2[i]]).astype(F32)
            q = _proj(hp, wq, q_gain_b[i], QK_SCALE * LOG2E)
            k32, kbf = _proj(hp, wk, k_gain_b[i], out32=True)
            v32, vbf = _proj(hp, wv, out32=True)
            o = _attn_b_prompt(q, kbf, vbf, rel_bias_table, lam_rows, sub_gain_b[i], lam_init)
            xp = _out_res(o, wo, xp, mp[2])
            st_p.append((k32.reshape(nb, seq, B_HEADS, 2, HEAD_DIM), v32.reshape(nb, seq, B_HEADS, B_VDIM)))
            q = _proj(hs, wq, q_gain_b[i], QK_SCALE).reshape(db, dt, nq)
            kn = _proj(hs, wk, k_gain_b[i], out32=True, outbf=False).reshape(db, dt, nq)
            vn = _proj(hs, wv, out32=True, outbf=False).reshape(db, dt, nq)
            o = _attn_b_sample(q, cache_b_k[i].reshape(db, past_len, nq), cache_b_v[i].reshape(db, past_len, nq),
                               kn, vn, rel_bias_table, lam_rows, sub_gain_b[i], lam_init, past_len)
            xs = _out_res(o.reshape(1, ns, d), wo, xs, ms[2])
            st_s.append((kn.reshape(db, dt, B_HEADS, 2, HEAD_DIM), vn.reshape(db, dt, B_HEADS, B_VDIM)))
        else:
            wq = w_in_c[i][:, :d].astype(BF16)
            wk = w_in_c[i][:, d:2 * d].astype(BF16)
            wv = w_in_c[i][:, 2 * d:].astype(BF16)
            wo = w_out_c[i].astype(BF16)
            q = _proj(hp, wq, None, QK_SCALE)
            k32, kbf = _proj(hp, wk, out32=True)
            v32, vbf = _proj(hp, wv, out32=True)
            o = _attn_c_prompt(q, kbf, vbf)
            xp = _out_res(o, wo, xp, mp[2])
            st_p.append((k32.reshape(nb, seq, C_HEADS, HEAD_DIM), v32.reshape(nb, seq, C_HEADS, HEAD_DIM)))
            q = _proj(hs, wq, None, QK_SCALE).reshape(db, dt, d)
            kn = _proj(hs, wk, out32=True, outbf=False).reshape(db, dt, d)
            vn = _proj(hs, wv, out32=True, outbf=False).reshape(db, dt, d)
            o = _attn_c_sample(q, cache_c_k[i].reshape(db, past_len, d), cache_c_v[i].reshape(db, past_len, d),
                               kn, vn)
            xs = _out_res(o.reshape(1, ns, d), wo, xs, ms[2])
            st_s.append((kn.reshape(db, dt, C_HEADS, HEAD_DIM), vn.reshape(db, dt, C_HEADS, HEAD_DIM)))

        w_cat, b_row = _router_weights(w_group[l], b_group[l], w_router[l], b_router[l])
        moe_w = (w_cat, b_row,
                 jnp.concatenate([w_gate[l], w_up[l]], axis=-1).astype(BF16),
                 w_down[l].astype(BF16))
        xp = _moe(xp, norm_ffn[l], mp[4], mp[3], mp[5], moe_w)
        xs = _moe(xs, norm_ffn[l], ms[4], ms[3], ms[5], moe_w)

    def collect(states, kind, j):
        parts = [states[l][j] for l in range(DEPTH) if l % N_MIXERS == kind]
        return parts[0][None] if len(parts) == 1 else jnp.stack(parts)

    return (xp, xs.reshape(db, dt, d),
            collect(st_p, 0, 0), collect(st_p, 0, 1),
            collect(st_p, 1, 0), collect(st_p, 1, 1),
            collect(st_p, 2, 0), collect(st_p, 2, 1),
            collect(st_s, 0, 0), collect(st_s, 0, 1),
            collect(st_s, 1, 0), collect(st_s, 1, 1),
            collect(st_s, 2, 0), collect(st_s, 2, 1))
```
